```python
import math
import jax, jax.numpy as jnp
from jax import lax
import numpy as np

D_MODEL = 4096
BATCH = 1
SEQ = 8192
DEPTH = 1
DEC_BATCH = 32
DEC_SEQ = 1
PAST_LEN = 8192
PAGE_SIZE = 128

ATTN_WIDTH = D_MODEL // 2
POOL_WIDTH = D_MODEL - ATTN_WIDTH
D_V = 128
N_HEADS = ATTN_WIDTH // D_V
D_SUB = D_V // 2
ROT_DIM = D_SUB // 4
ROPE_THETA = 500000.0
POOL_WINDOWS = (2, 4, 8, 16)
N_POOL_GROUPS = len(POOL_WINDOWS)
POOL_GROUP = POOL_WIDTH // N_POOL_GROUPS
POOL_BUF = max(POOL_WINDOWS) - 1
D_FF = 4 * D_MODEL
IN_WIDTH = 3 * ATTN_WIDTH + POOL_WIDTH
Q_BLOCK = 128
EPS = 1e-6
NEG_INF = -1e30

kernel_name = "hymba_diffattn_pool_decoder_step"


def rms_norm(x, g):
    x32 = x.astype(jnp.float32)
    y = x32 * lax.rsqrt(jnp.mean(x32 * x32, axis=-1, keepdims=True) + EPS)
    return (y * g.astype(jnp.float32)).astype(x.dtype)


def rope(x, pos):
    inv_freq = ROPE_THETA ** (-jnp.arange(0, ROT_DIM, 2, dtype=jnp.float32) / ROT_DIM)
    ang = pos.astype(jnp.float32)[:, None] * inv_freq[None, :]
    cos = jnp.cos(ang)[None, :, None, None, :]
    sin = jnp.sin(ang)[None, :, None, None, :]
    x32 = x.astype(jnp.float32)
    xr, xp = x32[..., :ROT_DIM], x32[..., ROT_DIM:]
    x1, x2 = xr[..., : ROT_DIM // 2], xr[..., ROT_DIM // 2:]
    out = jnp.concatenate([x1 * cos - x2 * sin, x2 * cos + x1 * sin, xp], axis=-1)
    return out.astype(x.dtype)


def project(h, w_in_l, q_g, k_g, pos):
    B, T, _ = h.shape
    z = h @ w_in_l
    q = z[..., :ATTN_WIDTH].reshape(B, T, N_HEADS, 2, D_SUB)
    k = z[..., ATTN_WIDTH:2 * ATTN_WIDTH].reshape(B, T, N_HEADS, 2, D_SUB)
    v = z[..., 2 * ATTN_WIDTH:3 * ATTN_WIDTH].reshape(B, T, N_HEADS, D_V)
    u = z[..., 3 * ATTN_WIDTH:]
    q = rope(rms_norm(q, q_g), pos)
    k = rope(rms_norm(k, k_g), pos)
    return q, k, v, u


def diff_lambda(lq1, lk1, lq2, lk2, lam_init):
    f = lambda a: a.astype(jnp.float32)
    return jnp.exp(jnp.sum(f(lq1) * f(lk1))) - jnp.exp(jnp.sum(f(lq2) * f(lk2))) + lam_init


def diff_probs(s, lam):
    p = jax.nn.softmax(s, axis=-1)
    return p[:, :, 0] - lam * p[:, :, 1]


def head_out(o, subln_g, lam_init, dtype):
    B, T = o.shape[:2]
    o = rms_norm(o, subln_g) * (1.0 - lam_init)
    return o.reshape(B, T, ATTN_WIDTH).astype(dtype)


def prompt_attention(q, k, v, pos, lam):
    B, S = q.shape[:2]
    nblk = S // Q_BLOCK
    qb = q.reshape(B, nblk, Q_BLOCK, N_HEADS, 2, D_SUB).transpose(1, 0, 2, 3, 4, 5)
    pb = pos.reshape(nblk, Q_BLOCK)
    v32 = v.astype(jnp.float32)
    scale = D_SUB ** -0.5

    def one(args):
        qi, pi = args
        s = jnp.einsum('bqhjd,bkhjd->bhjqk', qi, k, preferred_element_type=jnp.float32) * scale
        mask = pos[None, :] <= pi[:, None]
        s = jnp.where(mask, s, NEG_INF)
        p = diff_probs(s, lam)
        return jnp.einsum('bhqk,bkhd->bqhd', p, v32)

    o = lax.map(one, (qb, pb))
    return o.transpose(1, 0, 2, 3, 4).reshape(B, S, N_HEADS, D_V)


def sample_attention(q, k_new, v_new, k_past, v_past, lam):
    P = k_past.shape[1]
    DS = q.shape[1]
    scale = D_SUB ** -0.5
    s_past = jnp.einsum('bqhjd,bkhjd->bhjqk', q, k_past, preferred_element_type=jnp.float32) * scale
    s_new = jnp.einsum('bqhjd,bkhjd->bhjqk', q, k_new, preferred_element_type=jnp.float32) * scale
    tri = jnp.arange(DS)[None, :] <= jnp.arange(DS)[:, None]
    s_new = jnp.where(tri, s_new, NEG_INF)
    p = diff_probs(jnp.concatenate([s_past, s_new], axis=-1), lam)
    return (jnp.einsum('bhqk,bkhd->bqhd', p[..., :P], v_past.astype(jnp.float32))
            + jnp.einsum('bhqk,bkhd->bqhd', p[..., P:], v_new.astype(jnp.float32)))


def pool_mix(u_ext, pos_ext, n_out, w_l, scale_l):
    B, T, _ = u_ext.shape
    u32 = u_ext.astype(jnp.float32)
    cs = jnp.concatenate([jnp.zeros((B, 1, POOL_WIDTH), jnp.float32), jnp.cumsum(u32, axis=1)], axis=1)
    idx = jnp.arange(T - n_out, T)
    pos = pos_ext[T - n_out:]
    hi = jnp.take(cs, idx + 1, axis=1)
    cur = jnp.take(u32, idx, axis=1)
    outs = []
    for g, w in enumerate(POOL_WINDOWS):
        count = jnp.minimum(w, pos + 1)
        lo = jnp.take(cs[..., g * POOL_GROUP:(g + 1) * POOL_GROUP], idx + 1 - count, axis=1)
        mean = (hi[..., g * POOL_GROUP:(g + 1) * POOL_GROUP] - lo) / count.astype(jnp.float32)[None, :, None]
        outs.append(mean - cur[..., g * POOL_GROUP:(g + 1) * POOL_GROUP])
    d = jnp.stack(outs, axis=2)
    y = jnp.einsum('btgc,gcd->btgd', d, w_l.astype(jnp.float32)).reshape(B, n_out, POOL_WIDTH)
    return (y * scale_l.astype(jnp.float32)).astype(u_ext.dtype)


def finish(x, attn, pool, w_out_l, norm2_g_l, w_up_l, w_down_l):
    x = x + jnp.concatenate([attn, pool], axis=-1) @ w_out_l
    h = rms_norm(x, norm2_g_l)
    a = jax.nn.relu(h @ w_up_l)
    return x + (a * a) @ w_down_l


def setup_inputs(seed: int = 0) -> dict:
    key = jax.random.key(seed)
    ks = jax.random.split(key, 24)
    n_pages = PAST_LEN // PAGE_SIZE
    n_used = DEC_BATCH * n_pages
    n_phys = (n_used * 5) // 4
    nrm = jax.random.normal
    x_prompt = nrm(ks[0], (BATCH, SEQ, D_MODEL), jnp.float32)
    x_sample = nrm(ks[1], (DEC_BATCH, DEC_SEQ, D_MODEL), jnp.float32)
    cache_k = nrm(ks[2], (DEPTH, n_phys, PAGE_SIZE, N_HEADS, 2 * D_SUB), jnp.float32)
    cache_v = nrm(ks[3], (DEPTH, n_phys, PAGE_SIZE, N_HEADS, D_V), jnp.float32)
    state_pool = nrm(ks[4], (DEPTH, DEC_BATCH, POOL_BUF, POOL_WIDTH), jnp.float32)
    page_table = jax.random.permutation(ks[5], n_phys)[:n_used].reshape(DEC_BATCH, n_pages).astype(jnp.int32)
    gain = lambda k, shape: 1.0 + 0.1 * nrm(k, shape, jnp.float32)
    return {
        "x_prompt": x_prompt,
        "x_sample": x_sample,
        "cache_k": cache_k,
        "cache_v": cache_v,
        "state_pool": state_pool,
        "page_table": page_table,
        "norm1_g": gain(ks[6], (DEPTH, D_MODEL)),
        "w_in": nrm(ks[7], (DEPTH, D_MODEL, IN_WIDTH), jnp.float32) * D_MODEL ** -0.5,
        "q_norm_g": gain(ks[8], (DEPTH, D_SUB)),
        "k_norm_g": gain(ks[9], (DEPTH, D_SUB)),
        "lambda_q1": 0.1 * nrm(ks[10], (DEPTH, D_SUB), jnp.float32),
        "lambda_k1": 0.1 * nrm(ks[11], (DEPTH, D_SUB), jnp.float32),
        "lambda_q2": 0.1 * nrm(ks[12], (DEPTH, D_SUB), jnp.float32),
        "lambda_k2": 0.1 * nrm(ks[13], (DEPTH, D_SUB), jnp.float32),
        "subln_g": gain(ks[14], (DEPTH, D_V)),
        "pool_w": nrm(ks[15], (DEPTH, N_POOL_GROUPS, POOL_GROUP, POOL_GROUP), jnp.float32) * POOL_GROUP ** -0.5,
        "pool_scale": gain(ks[16], (DEPTH, POOL_WIDTH)),
        "w_out": nrm(ks[17], (DEPTH, D_MODEL, D_MODEL), jnp.float32) * D_MODEL ** -0.5,
        "norm2_g": gain(ks[18], (DEPTH, D_MODEL)),
        "w_up": nrm(ks[19], (DEPTH, D_MODEL, D_FF), jnp.float32) * D_MODEL ** -0.5,
        "w_down": nrm(ks[20], (DEPTH, D_FF, D_MODEL), jnp.float32) * D_FF ** -0.5,
    }


def reference(x_prompt, x_sample, cache_k, cache_v, state_pool, page_table, norm1_g, w_in,
              q_norm_g, k_norm_g, lambda_q1, lambda_k1, lambda_q2, lambda_k2, subln_g,
              pool_w, pool_scale, w_out, norm2_g, w_up, w_down):
    B, S = x_prompt.shape[:2]
    DB, DS = x_sample.shape[:2]
    past_len = page_table.shape[1] * PAGE_SIZE
    pos_p = jnp.arange(S, dtype=jnp.int32)
    pos_s = past_len + jnp.arange(DS, dtype=jnp.int32)
    pos_ext = jnp.arange(past_len - POOL_BUF, past_len + DS, dtype=jnp.int32)

    xp, xs = x_prompt, x_sample
    kp_l, vp_l, pp_l, ks_l, vs_l, ps_l = [], [], [], [], [], []
    for l in range(DEPTH):
        lam_init = 0.8 - 0.6 * math.exp(-0.3 * l)
        lam = diff_lambda(lambda_q1[l], lambda_k1[l], lambda_q2[l], lambda_k2[l], lam_init)

        h = rms_norm(xp, norm1_g[l])
        q, k, v, u = project(h, w_in[l], q_norm_g[l], k_norm_g[l], pos_p)
        attn = head_out(prompt_attention(q, k, v, pos_p, lam), subln_g[l], lam_init, xp.dtype)
        pool = pool_mix(u, pos_p, S, pool_w[l], pool_scale[l])
        xp = finish(xp, attn, pool, w_out[l], norm2_g[l], w_up[l], w_down[l])
        kp_l.append(k.reshape(B, S, N_HEADS, 2 * D_SUB))
        vp_l.append(v)
        pp_l.append(u[:, S - POOL_BUF:])

        hs = rms_norm(xs, norm1_g[l])
        qs, kn, vn, us = project(hs, w_in[l], q_norm_g[l], k_norm_g[l], pos_s)
        k_past = cache_k[l][page_table].reshape(DB, past_len, N_HEADS, 2, D_SUB)
        v_past = cache_v[l][page_table].reshape(DB, past_len, N_HEADS, D_V)
        attn_s = head_out(sample_attention(qs, kn, vn, k_past, v_past, lam), subln_g[l], lam_init, xs.dtype)
        u_ext = jnp.concatenate([state_pool[l].astype(us.dtype), us], axis=1)
        pool_s = pool_mix(u_ext, pos_ext, DS, pool_w[l], pool_scale[l])
        xs = finish(xs, attn_s, pool_s, w_out[l], norm2_g[l], w_up[l], w_down[l])
        ks_l.append(kn.reshape(DB, DS, N_HEADS, 2 * D_SUB))
        vs_l.append(vn)
        ps_l.append(u_ext[:, u_ext.shape[1] - POOL_BUF:])

    k_prompt = jnp.stack(kp_l)
    v_prompt = jnp.stack(vp_l)
    pool_prompt = jnp.stack(pp_l)
    k_sample = jnp.stack(ks_l)
    v_sample = jnp.stack(vs_l)
    pool_sample = jnp.stack(ps_l)
    return (xp, xs, k_prompt, v_prompt, pool_prompt, k_sample, v_sample, pool_sample)
```

```python
import functools
import math

import jax
import jax.numpy as jnp
from jax import lax
from jax.experimental import pallas as pl
from jax.experimental.pallas import tpu as pltpu

PAGE_SIZE = 128
D_V = 128
D_SUB = D_V // 2
ROT_DIM = D_SUB // 4
ROPE_THETA = 500000.0
POOL_WINDOWS = (2, 4, 8, 16)
POOL_BUF = max(POOL_WINDOWS) - 1
EPS = 1e-6
NEG_INF = -1e30

LANES = 128
MXU_DIM = 256
VMEM_LIMIT_BYTES = 56 * 1024 * 1024

F32 = jnp.float32
BF16 = jnp.bfloat16


def _params(*sem):
    return pltpu.CompilerParams(dimension_semantics=sem, vmem_limit_bytes=VMEM_LIMIT_BYTES)


def _rmsnorm_kernel(x_ref, g_ref, o_ref):
    x = x_ref[...]
    ms = jnp.mean(x * x, axis=-1, keepdims=True)
    o_ref[...] = (x * lax.rsqrt(ms + EPS) * g_ref[...]).astype(o_ref.dtype)


def rmsnorm(x, g, tm):
    m, d = x.shape
    return pl.pallas_call(
        _rmsnorm_kernel,
        grid=(m // tm,),
        in_specs=[pl.BlockSpec((tm, d), lambda i: (i, 0)),
                  pl.BlockSpec((1, d), lambda i: (0, 0))],
        out_specs=pl.BlockSpec((tm, d), lambda i: (i, 0)),
        out_shape=jax.ShapeDtypeStruct((m, d), BF16),
        compiler_params=_params("arbitrary"),
        name="rmsnorm",
    )(x, g.reshape(1, d))


def _rope_table_kernel(pos_ref, invf_ref, c_ref, s1_ref, s2_ref):
    ang = pos_ref[...] * invf_ref[...]
    c = jnp.cos(ang)
    s = jnp.sin(ang)
    r = lax.broadcasted_iota(jnp.int32, ang.shape, 1) & (D_SUB - 1)
    c_ref[...] = c
    s1_ref[...] = jnp.where(r < ROT_DIM // 2, -s, 0.0)
    s2_ref[...] = jnp.where((r >= ROT_DIM // 2) & (r < ROT_DIM), s, 0.0)


def rope_tables(pos, tm):
    m = pos.shape[0]
    inv_freq = ROPE_THETA ** (-jnp.arange(0, ROT_DIM, 2, dtype=F32) / ROT_DIM)
    sub = jnp.concatenate([inv_freq, inv_freq, jnp.zeros((D_SUB - ROT_DIM,), F32)])
    invf_lane = jnp.tile(sub, LANES // D_SUB).reshape(1, LANES)
    spec = pl.BlockSpec((tm, LANES), lambda i: (i, 0))
    shp = jax.ShapeDtypeStruct((m, LANES), F32)
    return pl.pallas_call(
        _rope_table_kernel,
        grid=(m // tm,),
        in_specs=[pl.BlockSpec((tm, 1), lambda i: (i, 0)),
                  pl.BlockSpec((1, LANES), lambda i: (0, 0))],
        out_specs=[spec, spec, spec],
        out_shape=[shp, shp, shp],
        compiler_params=_params("arbitrary"),
        name="rope_tables",
    )(pos, invf_lane)


def _qk_norm_rope(z, g_ref, bd_ref, c_ref, s1_ref, s2_ref, scale):
    tn = z.shape[1]
    zz = z * z
    hi = zz.astype(BF16)
    lo = (zz - hi.astype(F32)).astype(BF16)
    bd = bd_ref[...]
    parts = []
    for c in range(tn // MXU_DIM):
        sl = slice(c * MXU_DIM, (c + 1) * MXU_DIM)
        parts.append(jnp.dot(hi[:, sl], bd, preferred_element_type=F32)
                     + jnp.dot(lo[:, sl], bd, preferred_element_type=F32))
    ss = jnp.concatenate(parts, axis=1) if len(parts) > 1 else parts[0]
    y = z * lax.rsqrt(ss * (1.0 / D_SUB) + EPS) * g_ref[...]
    cos, s1, s2 = c_ref[...], s1_ref[...], s2_ref[...]
    outs = []
    for c in range(tn // LANES):
        ys = y[:, c * LANES:(c + 1) * LANES]
        up = pltpu.roll(ys, LANES - ROT_DIM // 2, axis=1)
        dn = pltpu.roll(ys, ROT_DIM // 2, axis=1)
        outs.append(ys * cos + up * s1 + dn * s2)
    out = jnp.concatenate(outs, axis=1) if len(outs) > 1 else outs[0]
    if scale != 1.0:
        out = out * scale
    return out


def _mm_fullk_kernel(*refs, n_lhs, epilogue, scale):
    x_refs = refs[:n_lhs]
    w_ref = refs[n_lhs]
    extra = refs[n_lhs + 1:-1]
    o_ref = refs[-1]
    kk = w_ref.shape[0] // n_lhs
    acc = None
    for i, x_ref in enumerate(x_refs):
        part = jnp.dot(x_ref[...], w_ref[i * kk:(i + 1) * kk, :], preferred_element_type=F32)
        acc = part if acc is None else acc + part
    if epilogue == "relu2":
        a = jnp.maximum(acc, 0.0)
        acc = a * a
    elif epilogue == "resid":
        acc = extra[0][...] + acc
    elif epilogue == "qk":
        acc = _qk_norm_rope(acc, *extra, scale)
    o_ref[...] = acc.astype(o_ref.dtype)


def mm_fullk(xs, w, col_off, n_cols, *, tm, tn, out_dtype, epilogue=None, extra=(), extra_specs=(),
             scale=1.0, name="mm"):
    m = xs[0].shape[0]
    k_total = w.shape[0]
    off = col_off // tn
    x_specs = [pl.BlockSpec((tm, x.shape[1]), lambda n, i: (i, 0)) for x in xs]
    w_spec = pl.BlockSpec((k_total, tn), lambda n, i: (0, n + off))
    return pl.pallas_call(
        functools.partial(_mm_fullk_kernel, n_lhs=len(xs), epilogue=epilogue, scale=scale),
        grid=(n_cols // tn, m // tm),
        in_specs=x_specs + [w_spec] + list(extra_specs),
        out_specs=pl.BlockSpec((tm, tn), lambda n, i: (i, n)),
        out_shape=jax.ShapeDtypeStruct((m, n_cols), out_dtype),
        compiler_params=_params("arbitrary", "arbitrary"),
        name=name,
    )(*xs, w, *extra)


def _mm_ktiled_kernel(x_ref, w_ref, r_ref, o_ref, acc_ref, *, nk):
    k = pl.program_id(2)

    @pl.when(k == 0)
    def _():
        acc_ref[...] = jnp.zeros_like(acc_ref)

    acc_ref[...] += jnp.dot(x_ref[...], w_ref[...], preferred_element_type=F32)

    @pl.when(k == nk - 1)
    def _():
        o_ref[...] = r_ref[...] + acc_ref[...]


def mm_ktiled_resid(x, w, resid, *, tm, tn, tk, name="mm_ktiled"):
    m, kdim = x.shape
    n = w.shape[1]
    nk = kdim // tk
    return pl.pallas_call(
        functools.partial(_mm_ktiled_kernel, nk=nk),
        grid=(m // tm, n // tn, nk),
        in_specs=[pl.BlockSpec((tm, tk), lambda i, j, k: (i, k)),
                  pl.BlockSpec((tk, tn), lambda i, j, k: (k, j)),
                  pl.BlockSpec((tm, tn), lambda i, j, k: (i, j))],
        out_specs=pl.BlockSpec((tm, tn), lambda i, j, k: (i, j)),
        out_shape=jax.ShapeDtypeStruct((m, n), F32),
        scratch_shapes=[pltpu.VMEM((tm, tn), F32)],
        compiler_params=_params("arbitrary", "arbitrary", "arbitrary"),
        name=name,
    )(x, w, resid)


def _diff_lambda(lq1_ref, lk1_ref, lq2_ref, lk2_ref, lam_init):
    a = jnp.sum(lq1_ref[...] * lk1_ref[...], axis=-1, keepdims=True)
    b = jnp.sum(lq2_ref[...] * lk2_ref[...], axis=-1, keepdims=True)
    return jnp.exp(a) - jnp.exp(b) + lam_init


def _flash_kernel(q_ref, k_ref, v_ref, lq1_ref, lk1_ref, lq2_ref, lk2_ref, sg_ref, o_ref,
                  kb_ref, vb_ref, m_ref, l_ref, acc_ref, *, tq, tk, lam_init):
    qi = pl.program_id(1)

    @pl.when(qi == 0)
    def _():
        kb_ref[...] = k_ref[...].astype(BF16)
        vb_ref[...] = v_ref[...].astype(BF16)

    q = q_ref[...].astype(F32)
    lane = lax.broadcasted_iota(jnp.int32, q.shape, 1)
    q2 = jnp.concatenate([jnp.where(lane < D_SUB, q, 0.0),
                          jnp.where(lane >= D_SUB, q, 0.0)], axis=0).astype(BF16)

    m_ref[...] = jnp.full_like(m_ref, NEG_INF)
    l_ref[...] = jnp.zeros_like(l_ref)
    acc_ref[...] = jnp.zeros_like(acc_ref)

    def step(j, masked):
        start = pl.multiple_of(j * tk, tk)
        kc = kb_ref[pl.ds(start, tk), :]
        s = lax.dot_general(q2, kc, (((1,), (1,)), ((), ())), preferred_element_type=F32)
        if masked:
            row = lax.broadcasted_iota(jnp.int32, s.shape, 0) & (tq - 1)
            col = lax.broadcasted_iota(jnp.int32, s.shape, 1)
            s = jnp.where(col + j * tk <= row + qi * tq, s, NEG_INF)
        m_prev = m_ref[...]
        m_new = jnp.maximum(m_prev, jnp.max(s, axis=1, keepdims=True))
        alpha = jnp.exp(m_prev - m_new)
        p = jnp.exp(s - m_new)
        l_ref[...] = alpha * l_ref[...] + jnp.sum(p, axis=1, keepdims=True)
        pv = jnp.dot(p.astype(BF16), vb_ref[pl.ds(start, tk), :], preferred_element_type=F32)
        acc_ref[...] = alpha * acc_ref[...] + pv
        m_ref[...] = m_new

    n_full = (qi * tq) // tk

    def body(j, carry):
        step(j, False)
        return carry

    lax.fori_loop(0, n_full, body, 0)
    step(n_full, True)

    acc = acc_ref[...]
    l = l_ref[...]
    lam = _diff_lambda(lq1_ref, lk1_ref, lq2_ref, lk2_ref, lam_init)
    o = acc[:tq] / l[:tq] - lam * (acc[tq:] / l[tq:])
    ms = jnp.mean(o * o, axis=-1, keepdims=True)
    y = o * lax.rsqrt(ms + EPS) * sg_ref[...] * (1.0 - lam_init)
    o_ref[...] = y.astype(o_ref.dtype)


def flash_diff_attention(q, k, v, lam_vecs, subln_g, lam_init, *, tq, tk):
    s, width = q.shape
    n_heads = width // D_V
    assert tk % tq == 0 and s % tk == 0
    vec = pl.BlockSpec((1, D_SUB), lambda h, i: (0, 0))
    return pl.pallas_call(
        functools.partial(_flash_kernel, tq=tq, tk=tk, lam_init=lam_init),
        grid=(n_heads, s // tq),
        in_specs=[pl.BlockSpec((tq, D_V), lambda h, i: (i, h)),
                  pl.BlockSpec((s, D_V), lambda h, i: (0, h)),
                  pl.BlockSpec((s, D_V), lambda h, i: (0, h)),
                  vec, vec, vec, vec,
                  pl.BlockSpec((1, D_V), lambda h, i: (0, 0))],
        out_specs=pl.BlockSpec((tq, D_V), lambda h, i: (i, h)),
        out_shape=jax.ShapeDtypeStruct((s, width), BF16),
        scratch_shapes=[pltpu.VMEM((s, D_V), BF16), pltpu.VMEM((s, D_V), BF16),
                        pltpu.VMEM((2 * tq, 1), F32), pltpu.VMEM((2 * tq, 1), F32),
                        pltpu.VMEM((2 * tq, D_V), F32)],
        compiler_params=_params("arbitrary", "arbitrary"),
        name="flash_diff_attention",
    )(q, k, v, *lam_vecs, subln_g.reshape(1, D_V))


def _paged_attn_kernel(pt_ref, q_ref, kn_ref, vn_ref, kc_ref, vc_ref, lq1_ref, lk1_ref, lq2_ref,
                       lk2_ref, sg_ref, o_ref, qrows_ref, m_ref, l_ref, acc_ref,
                       *, n_pages, n_heads, lam_init):
    del pt_ref
    p_idx = pl.program_id(1)
    width = n_heads * D_V
    rows = 2 * n_heads

    @pl.when(p_idx == 0)
    def _():
        q = jnp.broadcast_to(q_ref[0].astype(F32), (rows, width))
        row = lax.broadcasted_iota(jnp.int32, (rows, width), 0)
        lane = lax.broadcasted_iota(jnp.int32, (rows, width), 1)
        sub_of_row = row // n_heads
        head_of_row = row - sub_of_row * n_heads
        sel = (lane // D_V == head_of_row) & ((lane & (D_V - 1)) // D_SUB == sub_of_row)
        qrows_ref[...] = jnp.where(sel, q, 0.0).astype(BF16)
        m_ref[...] = jnp.full_like(m_ref, NEG_INF)
        l_ref[...] = jnp.zeros_like(l_ref)
        acc_ref[...] = jnp.zeros_like(acc_ref)

    kb = kc_ref[0].astype(BF16)
    s = lax.dot_general(qrows_ref[...], kb, (((1,), (1,)), ((), ())),
                        preferred_element_type=F32)
    m_prev = m_ref[...]
    m_new = jnp.maximum(m_prev, jnp.max(s, axis=1, keepdims=True))
    alpha = jnp.exp(m_prev - m_new)
    p = jnp.exp(s - m_new)
    l_ref[...] = alpha * l_ref[...] + jnp.sum(p, axis=1, keepdims=True)
    pv = jnp.dot(p.astype(BF16), vc_ref[0].astype(BF16), preferred_element_type=F32)
    acc_ref[...] = alpha * acc_ref[...] + pv
    m_ref[...] = m_new

    @pl.when(p_idx == n_pages - 1)
    def _():
        qf = qrows_ref[...].astype(F32)
        s_new = jnp.sum(qf * kn_ref[0], axis=1, keepdims=True)
        m_old = m_ref[...]
        m_fin = jnp.maximum(m_old, s_new)
        a = jnp.exp(m_old - m_fin)
        p_new = jnp.exp(s_new - m_fin)
        l = a * l_ref[...] + p_new
        acc = a * acc_ref[...] + p_new * vn_ref[0]
        lam = _diff_lambda(lq1_ref, lk1_ref, lq2_ref, lk2_ref, lam_init)
        o = acc[:n_heads] / l[:n_heads] - lam * (acc[n_heads:] / l[n_heads:])
        row = lax.broadcasted_iota(jnp.int32, o.shape, 0)
        lane = lax.broadcasted_iota(jnp.int32, o.shape, 1)
        o = jnp.where(lane // D_V == row, o, 0.0)
        ms = jnp.sum(o * o, axis=1, keepdims=True) * (1.0 / D_V)
        y = jnp.sum(o * lax.rsqrt(ms + EPS), axis=0, keepdims=True)
        y = y * sg_ref[...] * (1.0 - lam_init)
        o_ref[0] = y.astype(o_ref.dtype)


def paged_diff_attention(q, k_new, v_new, cache_k, cache_v, page_table, page_off, lam_vecs, subln_g,
                         lam_init):
    b, width = q.shape
    n_heads = width // D_V
    n_pages = page_table.shape[1]
    rows = 2 * n_heads
    row_spec = pl.BlockSpec((1, 1, width), lambda i, p, pt: (i, 0, 0))
    page_spec = pl.BlockSpec((1, PAGE_SIZE, width), lambda i, p, pt: (pt[i, p] + page_off, 0, 0))
    vec = pl.BlockSpec((1, D_SUB), lambda i, p, pt: (0, 0))
    out = pl.pallas_call(
        functools.partial(_paged_attn_kernel, n_pages=n_pages, n_heads=n_heads, lam_init=lam_init),
        grid_spec=pltpu.PrefetchScalarGridSpec(
            num_scalar_prefetch=1,
            grid=(b, n_pages),
            in_specs=[row_spec, row_spec, row_spec, page_spec, page_spec, vec, vec, vec, vec,
                      pl.BlockSpec((1, width), lambda i, p, pt: (0, 0))],
            out_specs=row_spec,
            scratch_shapes=[pltpu.VMEM((rows, width), BF16),
                            pltpu.VMEM((rows, 1), F32), pltpu.VMEM((rows, 1), F32),
                            pltpu.VMEM((rows, width), F32)]),
        out_shape=jax.ShapeDtypeStruct((b, 1, width), BF16),
        compiler_params=_params("arbitrary", "arbitrary"),
        name="paged_diff_attention",
    )(page_table, q.reshape(b, 1, width), k_new.reshape(b, 1, width), v_new.reshape(b, 1, width),
      cache_k, cache_v, *lam_vecs, jnp.tile(subln_g, n_heads).reshape(1, width))
    return out.reshape(b, width)


HALO = POOL_BUF + 1


def _pool_prompt_kernel(u_ref, halo_ref, w_ref, sc_ref, o_ref, ext_ref, *, tm, group):
    i = pl.program_id(0)
    halo = halo_ref[...]
    ext_ref[0:HALO, :] = jnp.where(i > 0, halo, jnp.zeros_like(halo))
    ext_ref[HALO:, :] = u_ref[...]
    pos = i * tm + lax.broadcasted_iota(jnp.int32, (tm, 1), 0)
    for g, win in enumerate(POOL_WINDOWS):
        cols = slice(g * group, (g + 1) * group)
        cur = u_ref[:, cols]
        tot = cur
        for k in range(1, win):
            tot = tot + ext_ref[HALO - k:HALO - k + tm, cols]
        cnt = jnp.minimum(win, pos + 1).astype(F32)
        d = tot / cnt - cur
        y = jnp.dot(d.astype(BF16), w_ref[g], preferred_element_type=F32) * sc_ref[:, cols]
        o_ref[:, cols] = y.astype(o_ref.dtype)


def pool_mix_prompt(u, pool_w, pool_scale, *, tm):
    s, width = u.shape
    group = width // len(POOL_WINDOWS)
    ratio = tm // HALO
    return pl.pallas_call(
        functools.partial(_pool_prompt_kernel, tm=tm, group=group),
        grid=(s // tm,),
        in_specs=[pl.BlockSpec((tm, width), lambda i: (i, 0)),
                  pl.BlockSpec((HALO, width), lambda i: (jnp.maximum(i * ratio - 1, 0), 0)),
                  pl.BlockSpec(pool_w.shape, lambda i: (0, 0, 0)),
                  pl.BlockSpec((1, width), lambda i: (0, 0))],
        out_specs=pl.BlockSpec((tm, width), lambda i: (i, 0)),
        out_shape=jax.ShapeDtypeStruct((s, width), BF16),
        scratch_shapes=[pltpu.VMEM((tm + HALO, width), F32)],
        compiler_params=_params("arbitrary"),
        name="pool_mix_prompt",
    )(u, u, pool_w, pool_scale.reshape(1, width))


def _pool_sample_kernel(st_ref, us_ref, w_ref, sc_ref, o_ref, *, group, first_pos):
    cur_all = us_ref[...]
    for g, win in enumerate(POOL_WINDOWS):
        cols = slice(g * group, (g + 1) * group)
        cur = cur_all[:, cols]
        tot = cur
        for k in range(1, win):
            tot = tot + st_ref[POOL_BUF - k, :, cols]
        cnt = float(min(win, first_pos + 1))
        d = tot / cnt - cur
        y = jnp.dot(d.astype(BF16), w_ref[g], preferred_element_type=F32) * sc_ref[:, cols]
        o_ref[:, cols] = y.astype(o_ref.dtype)


def pool_mix_sample(state_t, us, pool_w, pool_scale, first_pos):
    b, width = us.shape
    group = width // len(POOL_WINDOWS)
    return pl.pallas_call(
        functools.partial(_pool_sample_kernel, group=group, first_pos=first_pos),
        grid=(1,),
        in_specs=[pl.BlockSpec(state_t.shape, lambda i: (0, 0, 0)),
                  pl.BlockSpec((b, width), lambda i: (0, 0)),
                  pl.BlockSpec(pool_w.shape, lambda i: (0, 0, 0)),
                  pl.BlockSpec((1, width), lambda i: (0, 0))],
        out_specs=pl.BlockSpec((b, width), lambda i: (0, 0)),
        out_shape=jax.ShapeDtypeStruct((b, width), BF16),
        compiler_params=_params("arbitrary"),
        name="pool_mix_sample",
    )(state_t, us, pool_w, pool_scale.reshape(1, width))


def _block_diag_ones():
    r = jnp.arange(MXU_DIM) // D_SUB
    return (r[:, None] == r[None, :]).astype(BF16)


def _tile_lanes(vec, n):
    return jnp.tile(vec, n // vec.shape[0]).reshape(1, n)


def project(x, pos, wts, *, tm, tn):
    d = x.shape[1]
    attn_w = d // 2
    h = rmsnorm(x, wts["norm1_g"], min(tm, 256))
    cos, s1, s2 = rope_tables(pos, tm)
    tab_spec = pl.BlockSpec((tm, LANES), lambda n, i: (i, 0))
    vec_spec = pl.BlockSpec((1, tn), lambda n, i: (0, 0))
    bd_spec = pl.BlockSpec((MXU_DIM, MXU_DIM), lambda n, i: (0, 0))
    specs = (vec_spec, bd_spec, tab_spec, tab_spec, tab_spec)
    bd = _block_diag_ones()
    q = mm_fullk([h], wts["w_in"], 0, attn_w, tm=tm, tn=tn, out_dtype=BF16, epilogue="qk",
                 extra=(_tile_lanes(wts["q_norm_g"], tn), bd, cos, s1, s2), extra_specs=specs,
                 scale=D_SUB ** -0.5, name="proj_q")
    k = mm_fullk([h], wts["w_in"], attn_w, attn_w, tm=tm, tn=tn, out_dtype=F32, epilogue="qk",
                 extra=(_tile_lanes(wts["k_norm_g"], tn), bd, cos, s1, s2), extra_specs=specs,
                 name="proj_k")
    v = mm_fullk([h], wts["w_in"], 2 * attn_w, attn_w, tm=tm, tn=tn, out_dtype=F32, name="proj_v")
    u = mm_fullk([h], wts["w_in"], 3 * attn_w, d - attn_w, tm=tm, tn=tn, out_dtype=F32, name="proj_u")
    return q, k, v, u


def finish(x, attn, pool, wts, *, tm, tn, tm_down, tn_down, tk_down):
    d = x.shape[1]
    x1 = mm_fullk([attn, pool], wts["w_out"], 0, d, tm=tm, tn=tn, out_dtype=F32, epilogue="resid",
                  extra=(x,), extra_specs=(pl.BlockSpec((tm, tn), lambda n, i: (i, n)),),
                  name="out_proj")
    h2 = rmsnorm(x1, wts["norm2_g"], min(tm, 256))
    a = mm_fullk([h2], wts["w_up"], 0, wts["w_up"].shape[1], tm=tm, tn=tn, out_dtype=BF16,
                 epilogue="relu2", name="mlp_up")
    return mm_ktiled_resid(a, wts["w_down"], x1, tm=tm_down, tn=tn_down, tk=tk_down, name="mlp_down")


def kernel(x_prompt, x_sample, cache_k, cache_v, state_pool, page_table, norm1_g, w_in, q_norm_g,
           k_norm_g, lambda_q1, lambda_k1, lambda_q2, lambda_k2, subln_g, pool_w, pool_scale, w_out,
           norm2_g, w_up, w_down):
    batch, seq, d_model = x_prompt.shape
    dec_batch, dec_seq, _ = x_sample.shape
    assert batch == 1 and dec_seq == 1
    depth = w_in.shape[0]
    n_phys = cache_k.shape[1]
    n_heads = cache_k.shape[3]
    attn_w = n_heads * D_V
    pool_width = d_model - attn_w
    past_len = page_table.shape[1] * PAGE_SIZE

    xp = x_prompt.reshape(seq, d_model)
    xs = x_sample.reshape(dec_batch, d_model)
    pos_p = jnp.arange(seq, dtype=F32).reshape(seq, 1)
    pos_s = jnp.full((dec_batch, 1), past_len, F32)
    cache_k2 = cache_k.reshape(depth * n_phys, PAGE_SIZE, attn_w)
    cache_v2 = cache_v.reshape(depth * n_phys, PAGE_SIZE, attn_w)

    outs = [[] for _ in range(6)]
    for l in range(depth):
        lam_init = 0.8 - 0.6 * math.exp(-0.3 * l)
        wts = {
            "norm1_g": norm1_g[l], "q_norm_g": q_norm_g[l], "k_norm_g": k_norm_g[l],
            "norm2_g": norm2_g[l],
            "w_in": w_in[l].astype(BF16), "w_out": w_out[l].astype(BF16),
            "w_up": w_up[l].astype(BF16), "w_down": w_down[l].astype(BF16),
        }
        pool_w_l = pool_w[l].astype(BF16)
        lam_vecs = [v[l].reshape(1, D_SUB) for v in (lambda_q1, lambda_k1, lambda_q2, lambda_k2)]

        q, k, v, u = project(xp, pos_p, wts, tm=512, tn=512)
        attn = flash_diff_attention(q, k, v, lam_vecs, subln_g[l], lam_init, tq=256, tk=512)
        pool = pool_mix_prompt(u, pool_w_l, pool_scale[l], tm=256)
        xp = finish(xp, attn, pool, wts, tm=512, tn=512, tm_down=1024, tn_down=1024, tk_down=1024)
        outs[0].append(k.reshape(batch, seq, n_heads, D_V))
        outs[1].append(v.reshape(batch, seq, n_heads, D_V))
        outs[2].append(u[seq - POOL_BUF:].reshape(batch, POOL_BUF, pool_width))

        qs, kn, vn, us = project(xs, pos_s, wts, tm=dec_batch, tn=512)
        attn_s = paged_diff_attention(qs, kn, vn, cache_k2, cache_v2, page_table, l * n_phys,
                                      lam_vecs, subln_g[l], lam_init)
        state_l = state_pool[l]
        pool_s = pool_mix_sample(jnp.swapaxes(state_l, 0, 1), us, pool_w_l, pool_scale[l], past_len)
        xs = finish(xs, attn_s, pool_s, wts, tm=dec_batch, tn=512, tm_down=dec_batch, tn_down=1024,
                    tk_down=1024)
        outs[3].append(kn.reshape(dec_batch, dec_seq, n_heads, D_V))
        outs[4].append(vn.reshape(dec_batch, dec_seq, n_heads, D_V))
        outs[5].append(jnp.concatenate([state_l[:, 1:], us[:, None, :]], axis=1))

    k_prompt, v_prompt, pool_prompt, k_sample, v_sample, pool_sample = (jnp.stack(o) for o in outs)
    return (xp.reshape(batch, seq, d_model), xs.reshape(dec_batch, dec_seq, d_model),
            k_prompt, v_prompt, pool_prompt, k_sample, v_sample, pool_sample)
```

```python
import functools
import math

import jax
import jax.numpy as jnp
from jax import lax
from jax.experimental import pallas as pl
from jax.experimental.pallas import tpu as pltpu

PAGE_SIZE = 128
D_V = 128
D_SUB = D_V // 2
ROT_DIM = D_SUB // 4
ROPE_THETA = 500000.0
POOL_WINDOWS = (2, 4, 8, 16)
POOL_BUF = max(POOL_WINDOWS) - 1
EPS = 1e-6
NEG_INF = -1e30

LANES = 128
MXU_DIM = 256
VMEM_LIMIT_BYTES = 56 * 1024 * 1024

F32 = jnp.float32
BF16 = jnp.bfloat16


def _params(*sem):
    return pltpu.CompilerParams(dimension_semantics=sem, vmem_limit_bytes=VMEM_LIMIT_BYTES)


def _rmsnorm_kernel(x_ref, g_ref, o_ref):
    x = x_ref[...]
    ms = jnp.mean(x * x, axis=-1, keepdims=True)
    o_ref[...] = (x * lax.rsqrt(ms + EPS) * g_ref[...]).astype(o_ref.dtype)


def rmsnorm(x, g, tm):
    m, d = x.shape
    return pl.pallas_call(
        _rmsnorm_kernel,
        grid=(m // tm,),
        in_specs=[pl.BlockSpec((tm, d), lambda i: (i, 0)),
                  pl.BlockSpec((1, d), lambda i: (0, 0))],
        out_specs=pl.BlockSpec((tm, d), lambda i: (i, 0)),
        out_shape=jax.ShapeDtypeStruct((m, d), BF16),
        compiler_params=_params("arbitrary"),
        name="rmsnorm",
    )(x, g.reshape(1, d))


def _rope_table_kernel(pos_ref, invf_ref, c_ref, s1_ref, s2_ref):
    ang = pos_ref[...] * invf_ref[...]
    c = jnp.cos(ang)
    s = jnp.sin(ang)
    r = lax.broadcasted_iota(jnp.int32, ang.shape, 1) & (D_SUB - 1)
    c_ref[...] = c
    s1_ref[...] = jnp.where(r < ROT_DIM // 2, -s, 0.0)
    s2_ref[...] = jnp.where((r >= ROT_DIM // 2) & (r < ROT_DIM), s, 0.0)


def rope_tables(pos, tm):
    m = pos.shape[0]
    inv_freq = ROPE_THETA ** (-jnp.arange(0, ROT_DIM, 2, dtype=F32) / ROT_DIM)
    sub = jnp.concatenate([inv_freq, inv_freq, jnp.zeros((D_SUB - ROT_DIM,), F32)])
    invf_lane = jnp.tile(sub, LANES // D_SUB).reshape(1, LANES)
    spec = pl.BlockSpec((tm, LANES), lambda i: (i, 0))
    shp = jax.ShapeDtypeStruct((m, LANES), F32)
    return pl.pallas_call(
        _rope_table_kernel,
        grid=(m // tm,),
        in_specs=[pl.BlockSpec((tm, 1), lambda i: (i, 0)),
                  pl.BlockSpec((1, LANES), lambda i: (0, 0))],
        out_specs=[spec, spec, spec],
        out_shape=[shp, shp, shp],
        compiler_params=_params("arbitrary"),
        name="rope_tables",
    )(pos, invf_lane)


def _qk_norm_rope(z, g_ref, bd_ref, c_ref, s1_ref, s2_ref, scale):
    tn = z.shape[1]
    zz = z * z
    hi = zz.astype(BF16)
    lo = (zz - hi.astype(F32)).astype(BF16)
    bd = bd_ref[...]
    parts = []
    for c in range(tn // MXU_DIM):
        sl = slice(c * MXU_DIM, (c + 1) * MXU_DIM)
        parts.append(jnp.dot(hi[:, sl], bd, preferred_element_type=F32)
                     + jnp.dot(lo[:, sl], bd, preferred_element_type=F32))
    ss = jnp.concatenate(parts, axis=1) if len(parts) > 1 else parts[0]
    y = z * lax.rsqrt(ss * (1.0 / D_SUB) + EPS) * g_ref[...]
    cos, s1, s2 = c_ref[...], s1_ref[...], s2_ref[...]
    outs = []
    for c in range(tn // LANES):
        ys = y[:, c * LANES:(c + 1) * LANES]
        up = pltpu.roll(ys, LANES - ROT_DIM // 2, axis=1)
        dn = pltpu.roll(ys, ROT_DIM // 2, axis=1)
        outs.append(ys * cos + up * s1 + dn * s2)
    out = jnp.concatenate(outs, axis=1) if len(outs) > 1 else outs[0]
    if scale != 1.0:
        out = out * scale
    return out


def _mm_fullk_kernel(*refs, n_lhs, epilogue, scale):
    x_refs = refs[:n_lhs]
    w_ref = refs[n_lhs]
    extra = refs[n_lhs + 1:-1]
    o_ref = refs[-1]
    kk = w_ref.shape[0] // n_lhs
    acc = None
    for i, x_ref in enumerate(x_refs):
        part = jnp.dot(x_ref[...], w_ref[i * kk:(i + 1) * kk, :], preferred_element_type=F32)
        acc = part if acc is None else acc + part
    if epilogue == "relu2":
        a = jnp.maximum(acc, 0.0)
        acc = a * a
    elif epilogue == "resid":
        acc = extra[0][...] + acc
    elif epilogue == "qk":
        acc = _qk_norm_rope(acc, *extra, scale)
    o_ref[...] = acc.astype(o_ref.dtype)


def mm_fullk(xs, w, col_off, n_cols, *, tm, tn, out_dtype, epilogue=None, extra=(), extra_specs=(),
             scale=1.0, name="mm"):
    m = xs[0].shape[0]
    k_total = w.shape[0]
    off = col_off // tn
    x_specs = [pl.BlockSpec((tm, x.shape[1]), lambda n, i: (i, 0)) for x in xs]
    w_spec = pl.BlockSpec((k_total, tn), lambda n, i: (0, n + off))
    return pl.pallas_call(
        functools.partial(_mm_fullk_kernel, n_lhs=len(xs), epilogue=epilogue, scale=scale),
        grid=(n_cols // tn, m // tm),
        in_specs=x_specs + [w_spec] + list(extra_specs),
        out_specs=pl.BlockSpec((tm, tn), lambda n, i: (i, n)),
        out_shape=jax.ShapeDtypeStruct((m, n_cols), out_dtype),
        compiler_params=_params("arbitrary", "arbitrary"),
        name=name,
    )(*xs, w, *extra)


def _mm_ktiled_kernel(x_ref, w_ref, r_ref, o_ref):
    part = jnp.dot(x_ref[...], w_ref[...], preferred_element_type=F32)

    @pl.when(pl.program_id(2) == 0)
    def _():
        o_ref[...] = r_ref[...] + part

    @pl.when(pl.program_id(2) > 0)
    def _():
        o_ref[...] += part


def mm_ktiled_resid(x, w, resid, *, tm, tn, tk, name="mm_ktiled"):
    m, kdim = x.shape
    n = w.shape[1]
    nk = kdim // tk
    return pl.pallas_call(
        _mm_ktiled_kernel,
        grid=(m // tm, n // tn, nk),
        in_specs=[pl.BlockSpec((tm, tk), lambda i, j, k: (i, k)),
                  pl.BlockSpec((tk, tn), lambda i, j, k: (k, j)),
                  pl.BlockSpec((tm, tn), lambda i, j, k: (i, j))],
        out_specs=pl.BlockSpec((tm, tn), lambda i, j, k: (i, j)),
        out_shape=jax.ShapeDtypeStruct((m, n), F32),
        compiler_params=_params("arbitrary", "arbitrary", "arbitrary"),
        name=name,
    )(x, w, resid)


def _diff_lambda(lq1_ref, lk1_ref, lq2_ref, lk2_ref, lam_init):
    a = jnp.sum(lq1_ref[...] * lk1_ref[...], axis=-1, keepdims=True)
    b = jnp.sum(lq2_ref[...] * lk2_ref[...], axis=-1, keepdims=True)
    return jnp.exp(a) - jnp.exp(b) + lam_init


def _flash_kernel(q_ref, k_ref, v_ref, lq1_ref, lk1_ref, lq2_ref, lk2_ref, sg_ref, o_ref,
                  kb_ref, vt_ref, m_ref, l_ref, acc_ref, st_ref, *, tq, tk, lam_init):
    qi = pl.program_id(1)
    n_chunks = vt_ref.shape[0]

    @pl.when(qi == 0)
    def _():
        kb_ref[...] = k_ref[...].astype(BF16)
        for c in range(n_chunks):
            vt_ref[c] = v_ref[c * tk:(c + 1) * tk, :].T.astype(BF16)

    q = q_ref[...].astype(F32)
    lane = lax.broadcasted_iota(jnp.int32, q.shape, 1)
    q2 = jnp.concatenate([jnp.where(lane < D_SUB, q, 0.0),
                          jnp.where(lane >= D_SUB, q, 0.0)], axis=0).astype(BF16)

    m_ref[...] = jnp.full_like(m_ref, NEG_INF)
    l_ref[...] = jnp.zeros_like(l_ref)
    acc_ref[...] = jnp.zeros_like(acc_ref)

    def scores(j, slot):
        start = pl.multiple_of(j * tk, tk)
        kc = kb_ref[pl.ds(start, tk), :]
        st_ref[slot] = lax.dot_general(kc, q2, (((1,), (1,)), ((), ())),
                                       preferred_element_type=F32)

    def consume(j, slot, masked):
        st = st_ref[slot]
        if masked:
            key = lax.broadcasted_iota(jnp.int32, st.shape, 0) + j * tk
            qpos = (lax.broadcasted_iota(jnp.int32, st.shape, 1) & (tq - 1)) + qi * tq
            st = jnp.where(key <= qpos, st, NEG_INF)
        m_prev = m_ref[...]
        m_new = jnp.maximum(m_prev, jnp.max(st, axis=0, keepdims=True))
        alpha = jnp.exp(m_prev - m_new)
        pt = jnp.exp(st - m_new)
        l_ref[...] = alpha * l_ref[...] + jnp.sum(pt, axis=0, keepdims=True)
        pv = jnp.dot(vt_ref[j], pt.astype(BF16), preferred_element_type=F32)
        acc_ref[...] = alpha * acc_ref[...] + pv
        m_ref[...] = m_new

    n_full = (qi * tq) // tk
    scores(0, 0)

    def body(jj, carry):
        scores(2 * jj + 1, 1)
        consume(2 * jj, 0, False)
        scores(2 * jj + 2, 0)
        consume(2 * jj + 1, 1, False)
        return carry

    lax.fori_loop(0, n_full // 2, body, 0)

    @pl.when(n_full % 2 == 1)
    def _():
        scores(n_full, 1)
        consume(n_full - 1, 0, False)
        consume(n_full, 1, True)

    @pl.when(n_full % 2 == 0)
    def _():
        consume(n_full, 0, True)

    acc = acc_ref[...]
    l = l_ref[...]
    lam = _diff_lambda(lq1_ref, lk1_ref, lq2_ref, lk2_ref, lam_init)
    o = acc[:, :tq] / l[:, :tq] - lam * (acc[:, tq:] / l[:, tq:])
    ms = jnp.mean(o * o, axis=0, keepdims=True)
    y = o * lax.rsqrt(ms + EPS) * sg_ref[...] * (1.0 - lam_init)
    o_ref[...] = y.T.astype(o_ref.dtype)


def flash_diff_attention(q, k, v, lam_vecs, subln_g, lam_init, *, tq, tk):
    s, width = q.shape
    n_heads = width // D_V
    assert tk % tq == 0 and s % tk == 0
    vec = pl.BlockSpec((1, D_SUB), lambda h, i: (0, 0))
    return pl.pallas_call(
        functools.partial(_flash_kernel, tq=tq, tk=tk, lam_init=lam_init),
        grid=(n_heads, s // tq),
        in_specs=[pl.BlockSpec((tq, D_V), lambda h, i: (i, h)),
                  pl.BlockSpec((s, D_V), lambda h, i: (0, h)),
                  pl.BlockSpec((s, D_V), lambda h, i: (0, h)),
                  vec, vec, vec, vec,
                  pl.BlockSpec((D_V, 1), lambda h, i: (0, 0))],
        out_specs=pl.BlockSpec((tq, D_V), lambda h, i: (i, h)),
        out_shape=jax.ShapeDtypeStruct((s, width), BF16),
        scratch_shapes=[pltpu.VMEM((s, D_V), BF16), pltpu.VMEM((s // tk, D_V, tk), BF16),
                        pltpu.VMEM((1, 2 * tq), F32), pltpu.VMEM((1, 2 * tq), F32),
                        pltpu.VMEM((D_V, 2 * tq), F32), pltpu.VMEM((2, tk, 2 * tq), F32)],
        compiler_params=_params("arbitrary", "arbitrary"),
        name="flash_diff_attention",
    )(q, k, v, *lam_vecs, subln_g.reshape(D_V, 1))


def _paged_attn_kernel(pt_ref, q_ref, kn_ref, vn_ref, *rest, pp, n_steps, n_heads, lam_init):
    del pt_ref
    k_refs, v_refs = rest[:pp], rest[pp:2 * pp]
    (lq1_ref, lk1_ref, lq2_ref, lk2_ref, sg_ref, o_ref,
     q2_ref, bias_ref, m_ref, l_ref, acc_ref) = rest[2 * pp:]
    p_idx = pl.program_id(1)

    @pl.when(p_idx == 0)
    def _():
        q = q_ref[...].astype(F32)
        lane = lax.broadcasted_iota(jnp.int32, q.shape, 1)
        q2_ref[...] = jnp.concatenate([jnp.where(lane < D_SUB, q, 0.0),
                                       jnp.where(lane >= D_SUB, q, 0.0)], axis=0).astype(BF16)
        row = lax.broadcasted_iota(jnp.int32, bias_ref.shape, 0)
        col = lax.broadcasted_iota(jnp.int32, bias_ref.shape, 1)
        same_head = (col & (n_heads - 1)) == (row & (n_heads - 1))
        bias_ref[...] = jnp.where(same_head, 0.0, NEG_INF)
        m_ref[...] = jnp.full_like(m_ref, NEG_INF)
        l_ref[...] = jnp.zeros_like(l_ref)
        acc_ref[...] = jnp.zeros_like(acc_ref)

    q2 = q2_ref[...]
    bias = bias_ref[...]
    s_parts = [lax.dot_general(q2, k_ref[...].astype(BF16), (((1,), (1,)), ((), ())),
                               preferred_element_type=F32) + bias for k_ref in k_refs]
    m_prev = m_ref[...]
    m_new = m_prev
    for s in s_parts:
        m_new = jnp.maximum(m_new, jnp.max(s, axis=1, keepdims=True))
    alpha = jnp.exp(m_prev - m_new)
    l_new = alpha * l_ref[...]
    acc = alpha * acc_ref[...]
    for s, v_ref in zip(s_parts, v_refs):
        p = jnp.exp(s - m_new)
        l_new = l_new + jnp.sum(p, axis=1, keepdims=True)
        acc = acc + jnp.dot(p.astype(BF16), v_ref[...].astype(BF16), preferred_element_type=F32)
    l_ref[...] = l_new
    acc_ref[...] = acc
    m_ref[...] = m_new

    @pl.when(p_idx == n_steps - 1)
    def _():
        kn2 = jnp.concatenate([kn_ref[...], kn_ref[...]], axis=0)
        vn2 = jnp.concatenate([vn_ref[...], vn_ref[...]], axis=0)
        s_new = jnp.sum(q2_ref[...].astype(F32) * kn2, axis=1, keepdims=True)
        m_old = m_ref[...]
        m_fin = jnp.maximum(m_old, s_new)
        a = jnp.exp(m_old - m_fin)
        p_new = jnp.exp(s_new - m_fin)
        l = a * l_ref[...] + p_new
        acc_f = a * acc_ref[...] + p_new * vn2
        lam = _diff_lambda(lq1_ref, lk1_ref, lq2_ref, lk2_ref, lam_init)
        o = acc_f[:n_heads] / l[:n_heads] - lam * (acc_f[n_heads:] / l[n_heads:])
        ms = jnp.mean(o * o, axis=1, keepdims=True)
        y = o * lax.rsqrt(ms + EPS) * sg_ref[...] * (1.0 - lam_init)
        o_ref[...] = y.astype(o_ref.dtype)


def paged_diff_attention(q, k_new, v_new, cache_k, cache_v, page_table, layer, lam_vecs, subln_g,
                         lam_init, *, pages_per_step):
    b, width = q.shape
    n_heads = width // D_V
    pp = pages_per_step
    n_steps = page_table.shape[1] // pp
    rows = 2 * n_heads
    page_rows = PAGE_SIZE * n_heads
    row_spec = pl.BlockSpec((None, n_heads, D_V), lambda i, p, pt: (i, 0, 0))
    n_phys = cache_k.shape[1]
    cache_k = cache_k.reshape(-1, page_rows, D_V)
    cache_v = cache_v.reshape(-1, page_rows, D_V)
    page_specs = [
        pl.BlockSpec((None, page_rows, D_V),
                     lambda i, p, pt, c=c: (layer * n_phys + pt[i, p * pp + c], 0, 0))
        for c in range(pp)]
    vec = pl.BlockSpec((1, D_SUB), lambda i, p, pt: (0, 0))
    to_heads = lambda a: a.reshape(b, n_heads, D_V)
    out = pl.pallas_call(
        functools.partial(_paged_attn_kernel, pp=pp, n_steps=n_steps, n_heads=n_heads,
                          lam_init=lam_init),
        grid_spec=pltpu.PrefetchScalarGridSpec(
            num_scalar_prefetch=1,
            grid=(b, n_steps),
            in_specs=[row_spec, row_spec, row_spec] + page_specs + page_specs
                     + [vec, vec, vec, vec, pl.BlockSpec((1, D_V), lambda i, p, pt: (0, 0))],
            out_specs=row_spec,
            scratch_shapes=[pltpu.VMEM((rows, D_V), BF16),
                            pltpu.VMEM((rows, page_rows), F32),
                            pltpu.VMEM((rows, 1), F32), pltpu.VMEM((rows, 1), F32),
                            pltpu.VMEM((rows, D_V), F32)]),
        out_shape=jax.ShapeDtypeStruct((b, n_heads, D_V), BF16),
        compiler_params=_params("arbitrary", "arbitrary"),
        name="paged_diff_attention",
    )(page_table, to_heads(q), to_heads(k_new), to_heads(v_new),
      *([cache_k] * pp), *([cache_v] * pp), *lam_vecs, subln_g.reshape(1, D_V))
    return out.reshape(b, width)


HALO = POOL_BUF + 1


def _pool_prompt_kernel(u_ref, halo_ref, w_ref, sc_ref, o_ref, ext_ref, *, tm, group):
    i = pl.program_id(0)
    halo = halo_ref[...]
    ext_ref[0:HALO, :] = jnp.where(i > 0, halo, jnp.zeros_like(halo))
    ext_ref[HALO:, :] = u_ref[...]
    pos = i * tm + lax.broadcasted_iota(jnp.int32, (tm, 1), 0)
    for g, win in enumerate(POOL_WINDOWS):
        cols = slice(g * group, (g + 1) * group)
        cur = u_ref[:, cols]
        tot = cur
        for k in range(1, win):
            tot = tot + ext_ref[HALO - k:HALO - k + tm, cols]
        cnt = jnp.minimum(win, pos + 1).astype(F32)
        d = tot / cnt - cur
        y = jnp.dot(d.astype(BF16), w_ref[g], preferred_element_type=F32) * sc_ref[:, cols]
        o_ref[:, cols] = y.astype(o_ref.dtype)


def pool_mix_prompt(u, pool_w, pool_scale, *, tm):
    s, width = u.shape
    group = width // len(POOL_WINDOWS)
    ratio = tm // HALO
    return pl.pallas_call(
        functools.partial(_pool_prompt_kernel, tm=tm, group=group),
        grid=(s // tm,),
        in_specs=[pl.BlockSpec((tm, width), lambda i: (i, 0)),
                  pl.BlockSpec((HALO, width), lambda i: (jnp.maximum(i * ratio - 1, 0), 0)),
                  pl.BlockSpec(pool_w.shape, lambda i: (0, 0, 0)),
                  pl.BlockSpec((1, width), lambda i: (0, 0))],
        out_specs=pl.BlockSpec((tm, width), lambda i: (i, 0)),
        out_shape=jax.ShapeDtypeStruct((s, width), BF16),
        scratch_shapes=[pltpu.VMEM((tm + HALO, width), F32)],
        compiler_params=_params("arbitrary"),
        name="pool_mix_prompt",
    )(u, u, pool_w, pool_scale.reshape(1, width))


def _pool_sample_kernel(st_ref, us_ref, w_ref, sc_ref, o_ref, *, group, first_pos):
    cur_all = us_ref[...]
    for g, win in enumerate(POOL_WINDOWS):
        cols = slice(g * group, (g + 1) * group)
        cur = cur_all[:, cols]
        tot = cur
        for k in range(1, win):
            tot = tot + st_ref[POOL_BUF - k, :, cols]
        cnt = float(min(win, first_pos + 1))
        d = tot / cnt - cur
        y = jnp.dot(d.astype(BF16), w_ref[g], preferred_element_type=F32) * sc_ref[:, cols]
        o_ref[:, cols] = y.astype(o_ref.dtype)


def pool_mix_sample(state_t, us, pool_w, pool_scale, first_pos):
    b, width = us.shape
    group = width // len(POOL_WINDOWS)
    return pl.pallas_call(
        functools.partial(_pool_sample_kernel, group=group, first_pos=first_pos),
        grid=(1,),
        in_specs=[pl.BlockSpec(state_t.shape, lambda i: (0, 0, 0)),
                  pl.BlockSpec((b, width), lambda i: (0, 0)),
                  pl.BlockSpec(pool_w.shape, lambda i: (0, 0, 0)),
                  pl.BlockSpec((1, width), lambda i: (0, 0))],
        out_specs=pl.BlockSpec((b, width), lambda i: (0, 0)),
        out_shape=jax.ShapeDtypeStruct((b, width), BF16),
        compiler_params=_params("arbitrary"),
        name="pool_mix_sample",
    )(state_t, us, pool_w, pool_scale.reshape(1, width))


def _block_diag_ones():
    r = jnp.arange(MXU_DIM) // D_SUB
    return (r[:, None] == r[None, :]).astype(BF16)


def _tile_lanes(vec, n):
    return jnp.tile(vec, n // vec.shape[0]).reshape(1, n)


def project(x, pos, wts, *, tm, tn):
    d = x.shape[1]
    attn_w = d // 2
    h = rmsnorm(x, wts["norm1_g"], min(tm, 256))
    cos, s1, s2 = rope_tables(pos, tm)
    tab_spec = pl.BlockSpec((tm, LANES), lambda n, i: (i, 0))
    vec_spec = pl.BlockSpec((1, tn), lambda n, i: (0, 0))
    bd_spec = pl.BlockSpec((MXU_DIM, MXU_DIM), lambda n, i: (0, 0))
    specs = (vec_spec, bd_spec, tab_spec, tab_spec, tab_spec)
    bd = _block_diag_ones()
    q = mm_fullk([h], wts["w_in"], 0, attn_w, tm=tm, tn=tn, out_dtype=BF16, epilogue="qk",
                 extra=(_tile_lanes(wts["q_norm_g"], tn), bd, cos, s1, s2), extra_specs=specs,
                 scale=D_SUB ** -0.5, name="proj_q")
    k = mm_fullk([h], wts["w_in"], attn_w, attn_w, tm=tm, tn=tn, out_dtype=F32, epilogue="qk",
                 extra=(_tile_lanes(wts["k_norm_g"], tn), bd, cos, s1, s2), extra_specs=specs,
                 name="proj_k")
    v = mm_fullk([h], wts["w_in"], 2 * attn_w, attn_w, tm=tm, tn=tn, out_dtype=F32, name="proj_v")
    u = mm_fullk([h], wts["w_in"], 3 * attn_w, d - attn_w, tm=tm, tn=tn, out_dtype=F32, name="proj_u")
    return q, k, v, u


def finish(x, attn, pool, wts, *, tm, tn, tn_up, tm_down, tn_down, tk_down):
    d = x.shape[1]
    x1 = mm_fullk([attn, pool], wts["w_out"], 0, d, tm=tm, tn=tn, out_dtype=F32, epilogue="resid",
                  extra=(x,), extra_specs=(pl.BlockSpec((tm, tn), lambda n, i: (i, n)),),
                  name="out_proj")
    h2 = rmsnorm(x1, wts["norm2_g"], min(tm, 256))
    a = mm_fullk([h2], wts["w_up"], 0, wts["w_up"].shape[1], tm=tm, tn=tn_up, out_dtype=BF16,
                 epilogue="relu2", name="mlp_up")
    return mm_ktiled_resid(a, wts["w_down"], x1, tm=tm_down, tn=tn_down, tk=tk_down, name="mlp_down")


def kernel(x_prompt, x_sample, cache_k, cache_v, state_pool, page_table, norm1_g, w_in, q_norm_g,
           k_norm_g, lambda_q1, lambda_k1, lambda_q2, lambda_k2, subln_g, pool_w, pool_scale, w_out,
           norm2_g, w_up, w_down):
    batch, seq, d_model = x_prompt.shape
    dec_batch, dec_seq, _ = x_sample.shape
    assert batch == 1 and dec_seq == 1
    depth = w_in.shape[0]
    n_heads = cache_k.shape[3]
    attn_w = n_heads * D_V
    pool_width = d_model - attn_w
    past_len = page_table.shape[1] * PAGE_SIZE

    xp = x_prompt.reshape(seq, d_model)
    xs = x_sample.reshape(dec_batch, d_model)
    pos_p = jnp.arange(seq, dtype=F32).reshape(seq, 1)
    pos_s = jnp.full((dec_batch, 1), past_len, F32)

    outs = [[] for _ in range(6)]
    for l in range(depth):
        lam_init = 0.8 - 0.6 * math.exp(-0.3 * l)
        wts = {
            "norm1_g": norm1_g[l], "q_norm_g": q_norm_g[l], "k_norm_g": k_norm_g[l],
            "norm2_g": norm2_g[l],
            "w_in": w_in[l].astype(BF16), "w_out": w_out[l].astype(BF16),
            "w_up": w_up[l].astype(BF16), "w_down": w_down[l].astype(BF16),
        }
        pool_w_l = pool_w[l].astype(BF16)
        lam_vecs = [v[l].reshape(1, D_SUB) for v in (lambda_q1, lambda_k1, lambda_q2, lambda_k2)]

        q, k, v, u = project(xp, pos_p, wts, tm=512, tn=512)
        attn = flash_diff_attention(q, k, v, lam_vecs, subln_g[l], lam_init, tq=512, tk=512)
        pool = pool_mix_prompt(u, pool_w_l, pool_scale[l], tm=256)
        xp = finish(xp, attn, pool, wts, tm=512, tn=1024, tn_up=1024, tm_down=1024, tn_down=1024,
                    tk_down=2048)
        outs[0].append(k.reshape(batch, seq, n_heads, D_V))
        outs[1].append(v.reshape(batch, seq, n_heads, D_V))
        outs[2].append(u[seq - POOL_BUF:].reshape(batch, POOL_BUF, pool_width))

        qs, kn, vn, us = project(xs, pos_s, wts, tm=dec_batch, tn=512)
        attn_s = paged_diff_attention(qs, kn, vn, cache_k, cache_v, page_table, l,
                                      lam_vecs, subln_g[l], lam_init, pages_per_step=4)
        state_l = state_pool[l]
        pool_s = pool_mix_sample(jnp.swapaxes(state_l, 0, 1), us, pool_w_l, pool_scale[l], past_len)
        xs = finish(xs, attn_s, pool_s, wts, tm=dec_batch, tn=1024, tn_up=1024, tm_down=dec_batch,
                    tn_down=2048, tk_down=2048)
        outs[3].append(kn.reshape(dec_batch, dec_seq, n_heads, D_V))
        outs[4].append(vn.reshape(dec_batch, dec_seq, n_heads, D_V))
        outs[5].append(jnp.concatenate([state_l[:, 1:], us[:, None, :]], axis=1))

    k_prompt, v_prompt, pool_prompt, k_sample, v_sample, pool_sample = (jnp.stack(o) for o in outs)
    return (xp.reshape(batch, seq, d_model), xs.reshape(dec_batch, dec_seq, d_model),
            k_prompt, v_prompt, pool_prompt, k_sample, v_sample, pool_sample)
```

```python
import functools
import math

import jax
import jax.numpy as jnp
from jax import lax
from jax.experimental import pallas as pl
from jax.experimental.pallas import tpu as pltpu

PAGE_SIZE = 128
D_V = 128
D_SUB = D_V // 2
ROT_DIM = D_SUB // 4
ROPE_THETA = 500000.0
POOL_WINDOWS = (2, 4, 8, 16)
POOL_BUF = max(POOL_WINDOWS) - 1
EPS = 1e-6
NEG_INF = -1e30

LANES = 128
MXU_DIM = 256
ONES_ROWS = 16
VMEM_LIMIT_BYTES = 56 * 1024 * 1024

F32 = jnp.float32
BF16 = jnp.bfloat16


def _params(*sem):
    return pltpu.CompilerParams(dimension_semantics=sem, vmem_limit_bytes=VMEM_LIMIT_BYTES)


def _rmsnorm_kernel(x_ref, g_ref, o_ref):
    x = x_ref[...]
    ms = jnp.mean(x * x, axis=-1, keepdims=True)
    o_ref[...] = (x * lax.rsqrt(ms + EPS) * g_ref[...]).astype(o_ref.dtype)


def rmsnorm(x, g, tm):
    m, d = x.shape
    return pl.pallas_call(
        _rmsnorm_kernel,
        grid=(m // tm,),
        in_specs=[pl.BlockSpec((tm, d), lambda i: (i, 0)),
                  pl.BlockSpec((1, d), lambda i: (0, 0))],
        out_specs=pl.BlockSpec((tm, d), lambda i: (i, 0)),
        out_shape=jax.ShapeDtypeStruct((m, d), BF16),
        compiler_params=_params("arbitrary"),
        name="rmsnorm",
    )(x, g.reshape(1, d))


def _rope_table_kernel(pos_ref, invf_ref, c_ref, s1_ref, s2_ref):
    ang = pos_ref[...] * invf_ref[...]
    c = jnp.cos(ang)
    s = jnp.sin(ang)
    r = lax.broadcasted_iota(jnp.int32, ang.shape, 1) & (D_SUB - 1)
    c_ref[...] = c
    s1_ref[...] = jnp.where(r < ROT_DIM // 2, -s, 0.0)
    s2_ref[...] = jnp.where((r >= ROT_DIM // 2) & (r < ROT_DIM), s, 0.0)


def rope_tables(pos, tm):
    m = pos.shape[0]
    inv_freq = ROPE_THETA ** (-jnp.arange(0, ROT_DIM, 2, dtype=F32) / ROT_DIM)
    sub = jnp.concatenate([inv_freq, inv_freq, jnp.zeros((D_SUB - ROT_DIM,), F32)])
    invf_lane = jnp.tile(sub, LANES // D_SUB).reshape(1, LANES)
    spec = pl.BlockSpec((tm, LANES), lambda i: (i, 0))
    shp = jax.ShapeDtypeStruct((m, LANES), F32)
    return pl.pallas_call(
        _rope_table_kernel,
        grid=(m // tm,),
        in_specs=[pl.BlockSpec((tm, 1), lambda i: (i, 0)),
                  pl.BlockSpec((1, LANES), lambda i: (0, 0))],
        out_specs=[spec, spec, spec],
        out_shape=[shp, shp, shp],
        compiler_params=_params("arbitrary"),
        name="rope_tables",
    )(pos, invf_lane)


def _qk_norm_rope(z, g_ref, bd_ref, c_ref, s1_ref, s2_ref, scale):
    tn = z.shape[1]
    zz = z * z
    hi = zz.astype(BF16)
    lo = (zz - hi.astype(F32)).astype(BF16)
    bd = bd_ref[...]
    parts = []
    for c in range(tn // MXU_DIM):
        sl = slice(c * MXU_DIM, (c + 1) * MXU_DIM)
        parts.append(jnp.dot(hi[:, sl], bd, preferred_element_type=F32)
                     + jnp.dot(lo[:, sl], bd, preferred_element_type=F32))
    ss = jnp.concatenate(parts, axis=1) if len(parts) > 1 else parts[0]
    y = z * lax.rsqrt(ss * (1.0 / D_SUB) + EPS) * g_ref[...]
    cos, s1, s2 = c_ref[...], s1_ref[...], s2_ref[...]
    outs = []
    for c in range(tn // LANES):
        ys = y[:, c * LANES:(c + 1) * LANES]
        up = pltpu.roll(ys, LANES - ROT_DIM // 2, axis=1)
        dn = pltpu.roll(ys, ROT_DIM // 2, axis=1)
        outs.append(ys * cos + up * s1 + dn * s2)
    out = jnp.concatenate(outs, axis=1) if len(outs) > 1 else outs[0]
    if scale != 1.0:
        out = out * scale
    return out


def _mm_fullk_kernel(*refs, n_lhs, epilogue, scale):
    x_refs = refs[:n_lhs]
    w_ref = refs[n_lhs]
    extra = refs[n_lhs + 1:-1]
    o_ref = refs[-1]
    kk = w_ref.shape[0] // n_lhs
    acc = None
    for i, x_ref in enumerate(x_refs):
        part = jnp.dot(x_ref[...], w_ref[i * kk:(i + 1) * kk, :], preferred_element_type=F32)
        acc = part if acc is None else acc + part
    if epilogue == "relu2":
        a = jnp.maximum(acc, 0.0)
        acc = a * a
    elif epilogue == "resid":
        acc = extra[0][...] + acc
    elif epilogue == "qk":
        acc = _qk_norm_rope(acc, *extra, scale)
    o_ref[...] = acc.astype(o_ref.dtype)


def mm_fullk(xs, w, col_off, n_cols, *, tm, tn, out_dtype, epilogue=None, extra=(), extra_specs=(),
             scale=1.0, name="mm"):
    m = xs[0].shape[0]
    k_total = w.shape[0]
    off = col_off // tn
    x_specs = [pl.BlockSpec((tm, x.shape[1]), lambda n, i: (i, 0)) for x in xs]
    w_spec = pl.BlockSpec((k_total, tn), lambda n, i: (0, n + off))
    return pl.pallas_call(
        functools.partial(_mm_fullk_kernel, n_lhs=len(xs), epilogue=epilogue, scale=scale),
        grid=(n_cols // tn, m // tm),
        in_specs=x_specs + [w_spec] + list(extra_specs),
        out_specs=pl.BlockSpec((tm, tn), lambda n, i: (i, n)),
        out_shape=jax.ShapeDtypeStruct((m, n_cols), out_dtype),
        compiler_params=_params("arbitrary", "arbitrary"),
        name=name,
    )(*xs, w, *extra)


def _mm_ktiled_kernel(x_ref, w_ref, r_ref, o_ref):
    part = jnp.dot(x_ref[...], w_ref[...], preferred_element_type=F32)

    @pl.when(pl.program_id(2) == 0)
    def _():
        o_ref[...] = r_ref[...] + part

    @pl.when(pl.program_id(2) > 0)
    def _():
        o_ref[...] += part


def mm_ktiled_resid(x, w, resid, *, tm, tn, tk, name="mm_ktiled"):
    m, kdim = x.shape
    n = w.shape[1]
    nk = kdim // tk
    return pl.pallas_call(
        _mm_ktiled_kernel,
        grid=(m // tm, n // tn, nk),
        in_specs=[pl.BlockSpec((tm, tk), lambda i, j, k: (i, k)),
                  pl.BlockSpec((tk, tn), lambda i, j, k: (k, j)),
                  pl.BlockSpec((tm, tn), lambda i, j, k: (i, j))],
        out_specs=pl.BlockSpec((tm, tn), lambda i, j, k: (i, j)),
        out_shape=jax.ShapeDtypeStruct((m, n), F32),
        compiler_params=_params("arbitrary", "arbitrary", "arbitrary"),
        name=name,
    )(x, w, resid)


def _diff_lambda(lq1_ref, lk1_ref, lq2_ref, lk2_ref, lam_init):
    a = jnp.sum(lq1_ref[...] * lk1_ref[...], axis=-1, keepdims=True)
    b = jnp.sum(lq2_ref[...] * lk2_ref[...], axis=-1, keepdims=True)
    return jnp.exp(a) - jnp.exp(b) + lam_init


def _flash_kernel(q_ref, k_ref, v_ref, lq1_ref, lk1_ref, lq2_ref, lk2_ref, sg_ref, o_ref,
                  kb_ref, vt_ref, m_ref, acc_ref, st_ref, *, tq, tk, lam_init):
    qi = pl.program_id(1)
    n_chunks = vt_ref.shape[0]

    @pl.when(qi == 0)
    def _():
        kb_ref[...] = k_ref[...].astype(BF16)
        vt_ref[:, D_V:, :] = jnp.ones((n_chunks, ONES_ROWS, tk), BF16)
        for c in range(n_chunks):
            vt_ref[c, :D_V, :] = v_ref[c * tk:(c + 1) * tk, :].T.astype(BF16)

    q = q_ref[...].astype(F32)
    lane = lax.broadcasted_iota(jnp.int32, q.shape, 1)
    q2 = jnp.concatenate([jnp.where(lane < D_SUB, q, 0.0),
                          jnp.where(lane >= D_SUB, q, 0.0)], axis=0).astype(BF16)

    m_ref[...] = jnp.full_like(m_ref, NEG_INF)
    acc_ref[...] = jnp.zeros_like(acc_ref)

    def scores(j, slot):
        start = pl.multiple_of(j * tk, tk)
        kc = kb_ref[pl.ds(start, tk), :]
        st_ref[slot] = lax.dot_general(kc, q2, (((1,), (1,)), ((), ())),
                                       preferred_element_type=F32)

    def consume(j, slot, masked):
        st = st_ref[slot]
        if masked:
            key = lax.broadcasted_iota(jnp.int32, st.shape, 0) + j * tk
            qpos = (lax.broadcasted_iota(jnp.int32, st.shape, 1) & (tq - 1)) + qi * tq
            st = jnp.where(key <= qpos, st, NEG_INF)
        m_prev = m_ref[...]
        m_new = jnp.maximum(m_prev, jnp.max(st, axis=0, keepdims=True))
        alpha = jnp.exp(m_prev - m_new)
        pt = jnp.exp(st - m_new)
        pv = jnp.dot(vt_ref[j], pt.astype(BF16), preferred_element_type=F32)
        acc_ref[...] = alpha * acc_ref[...] + pv
        m_ref[...] = m_new

    n_full = (qi * tq) // tk
    scores(0, 0)

    def body(jj, carry):
        scores(2 * jj + 1, 1)
        consume(2 * jj, 0, False)
        scores(2 * jj + 2, 0)
        consume(2 * jj + 1, 1, False)
        return carry

    lax.fori_loop(0, n_full // 2, body, 0)

    @pl.when(n_full % 2 == 1)
    def _():
        scores(n_full, 1)
        consume(n_full - 1, 0, False)
        consume(n_full, 1, True)

    @pl.when(n_full % 2 == 0)
    def _():
        consume(n_full, 0, True)

    acc = acc_ref[:D_V, :]
    l = acc_ref[D_V:D_V + 1, :]
    lam = _diff_lambda(lq1_ref, lk1_ref, lq2_ref, lk2_ref, lam_init)
    o = acc[:, :tq] / l[:, :tq] - lam * (acc[:, tq:] / l[:, tq:])
    ms = jnp.mean(o * o, axis=0, keepdims=True)
    y = o * lax.rsqrt(ms + EPS) * sg_ref[...] * (1.0 - lam_init)
    o_ref[...] = y.T.astype(o_ref.dtype)


def flash_diff_attention(q, k, v, lam_vecs, subln_g, lam_init, *, tq, tk):
    s, width = q.shape
    n_heads = width // D_V
    assert tk % tq == 0 and s % tk == 0
    vec = pl.BlockSpec((1, D_SUB), lambda h, i: (0, 0))
    return pl.pallas_call(
        functools.partial(_flash_kernel, tq=tq, tk=tk, lam_init=lam_init),
        grid=(n_heads, s // tq),
        in_specs=[pl.BlockSpec((tq, D_V), lambda h, i: (i, h)),
                  pl.BlockSpec((s, D_V), lambda h, i: (0, h)),
                  pl.BlockSpec((s, D_V), lambda h, i: (0, h)),
                  vec, vec, vec, vec,
                  pl.BlockSpec((D_V, 1), lambda h, i: (0, 0))],
        out_specs=pl.BlockSpec((tq, D_V), lambda h, i: (i, h)),
        out_shape=jax.ShapeDtypeStruct((s, width), BF16),
        scratch_shapes=[pltpu.VMEM((s, D_V), BF16),
                        pltpu.VMEM((s // tk, D_V + ONES_ROWS, tk), BF16),
                        pltpu.VMEM((1, 2 * tq), F32),
                        pltpu.VMEM((D_V + ONES_ROWS, 2 * tq), F32),
                        pltpu.VMEM((2, tk, 2 * tq), F32)],
        compiler_params=_params("arbitrary", "arbitrary"),
        name="flash_diff_attention",
    )(q, k, v, *lam_vecs, subln_g.reshape(D_V, 1))


def _paged_update(p_idx, n_steps, q_ref, kn_ref, vn_ref, k_refs, v_refs, lam_refs, sg_ref, o_ref,
                  scratch, *, n_heads, lam_init):
    lq1_ref, lk1_ref, lq2_ref, lk2_ref = lam_refs
    q2_ref, bias_ref, m_ref, l_ref, acc_ref = scratch

    @pl.when(p_idx == 0)
    def _():
        q = q_ref[...].astype(F32)
        lane = lax.broadcasted_iota(jnp.int32, q.shape, 1)
        q2_ref[...] = jnp.concatenate([jnp.where(lane < D_SUB, q, 0.0),
                                       jnp.where(lane >= D_SUB, q, 0.0)], axis=0).astype(BF16)
        row = lax.broadcasted_iota(jnp.int32, bias_ref.shape, 0)
        col = lax.broadcasted_iota(jnp.int32, bias_ref.shape, 1)
        same_head = (col & (n_heads - 1)) == (row & (n_heads - 1))
        bias_ref[...] = jnp.where(same_head, 0.0, NEG_INF)
        m_ref[...] = jnp.full_like(m_ref, NEG_INF)
        l_ref[...] = jnp.zeros_like(l_ref)
        acc_ref[...] = jnp.zeros_like(acc_ref)

    q2 = q2_ref[...]
    bias = bias_ref[...]
    s_parts = [lax.dot_general(q2, k_ref[...].astype(BF16), (((1,), (1,)), ((), ())),
                               preferred_element_type=F32) + bias for k_ref in k_refs]
    m_prev = m_ref[...]
    m_new = m_prev
    for s in s_parts:
        m_new = jnp.maximum(m_new, jnp.max(s, axis=1, keepdims=True))
    alpha = jnp.exp(m_prev - m_new)
    l_new = alpha * l_ref[...]
    acc = alpha * acc_ref[...]
    for s, v_ref in zip(s_parts, v_refs):
        p = jnp.exp(s - m_new)
        l_new = l_new + jnp.sum(p, axis=1, keepdims=True)
        acc = acc + jnp.dot(p.astype(BF16), v_ref[...].astype(BF16), preferred_element_type=F32)
    l_ref[...] = l_new
    acc_ref[...] = acc
    m_ref[...] = m_new

    @pl.when(p_idx == n_steps - 1)
    def _():
        kn2 = jnp.concatenate([kn_ref[...], kn_ref[...]], axis=0)
        vn2 = jnp.concatenate([vn_ref[...], vn_ref[...]], axis=0)
        s_new = jnp.sum(q2_ref[...].astype(F32) * kn2, axis=1, keepdims=True)
        m_old = m_ref[...]
        m_fin = jnp.maximum(m_old, s_new)
        a = jnp.exp(m_old - m_fin)
        p_new = jnp.exp(s_new - m_fin)
        l = a * l_ref[...] + p_new
        acc_f = a * acc_ref[...] + p_new * vn2
        lam = _diff_lambda(lq1_ref, lk1_ref, lq2_ref, lk2_ref, lam_init)
        o = acc_f[:n_heads] / l[:n_heads] - lam * (acc_f[n_heads:] / l[n_heads:])
        ms = jnp.mean(o * o, axis=1, keepdims=True)
        y = o * lax.rsqrt(ms + EPS) * sg_ref[...] * (1.0 - lam_init)
        o_ref[...] = y.astype(o_ref.dtype)


def _mlp_paged_kernel(pt_ref, *refs, host, grid_dims, pp, steps_per_seq, n_heads, lam_init):
    del pt_ref
    n_host = 2 if host == "up" else 3
    host_in, rest = refs[:n_host], refs[n_host:]
    q_ref, kn_ref, vn_ref = rest[:3]
    k_refs, v_refs = rest[3:3 + pp], rest[3 + pp:3 + 2 * pp]
    lam_refs = rest[3 + 2 * pp:7 + 2 * pp]
    sg_ref, o_ref, attn_ref = rest[7 + 2 * pp:10 + 2 * pp]
    scratch = rest[10 + 2 * pp:]

    part = jnp.dot(host_in[0][...], host_in[1][...], preferred_element_type=F32)
    if host == "up":
        a = jnp.maximum(part, 0.0)
        o_ref[...] = (a * a).astype(o_ref.dtype)
    else:
        @pl.when(pl.program_id(2) == 0)
        def _():
            o_ref[...] = host_in[2][...] + part

        @pl.when(pl.program_id(2) > 0)
        def _():
            o_ref[...] += part

    step = pl.program_id(0)
    for axis in range(1, len(grid_dims)):
        step = step * grid_dims[axis] + pl.program_id(axis)
    _paged_update(lax.rem(step, steps_per_seq), steps_per_seq, q_ref, kn_ref, vn_ref, k_refs, v_refs,
                  lam_refs, sg_ref, attn_ref, scratch, n_heads=n_heads, lam_init=lam_init)


def mlp_matmul_with_paged_attention(host, x, w, resid, sample, seq_lo, n_seq, *, tm, tn, tk,
                                    pages_per_step):
    m, kdim = x.shape
    n = w.shape[1]
    pp = pages_per_step
    n_heads = sample["n_heads"]
    page_table = sample["page_table"]
    steps_per_seq = page_table.shape[1] // pp
    rows = 2 * n_heads
    page_rows = PAGE_SIZE * n_heads
    if host == "up":
        grid = (n // tn, m // tm)
        assert tk == kdim
        host_specs = [pl.BlockSpec((tm, kdim), lambda j, i, pt: (i, 0)),
                      pl.BlockSpec((kdim, tn), lambda j, i, pt: (0, j))]
        host_out = pl.BlockSpec((tm, tn), lambda j, i, pt: (i, j))
        host_args, out_dtype = (x, w), BF16
    else:
        grid = (m // tm, n // tn, kdim // tk)
        host_specs = [pl.BlockSpec((tm, tk), lambda i, j, k, pt: (i, k)),
                      pl.BlockSpec((tk, tn), lambda i, j, k, pt: (k, j)),
                      pl.BlockSpec((tm, tn), lambda i, j, k, pt: (i, j))]
        host_out = pl.BlockSpec((tm, tn), lambda i, j, k, pt: (i, j))
        host_args, out_dtype = (x, w, resid), F32
    n_steps = math.prod(grid)
    assert n_steps == n_seq * steps_per_seq, (grid, n_seq, steps_per_seq)

    def linear(ids):
        step = ids[0]
        for axis in range(1, len(grid)):
            step = step * grid[axis] + ids[axis]
        return step

    def seq_map(*args):
        return (seq_lo + linear(args[:-1]) // steps_per_seq, 0, 0)

    def page_map(c):
        def index(*args):
            step, pt = linear(args[:-1]), args[-1]
            seq = seq_lo + step // steps_per_seq
            return (sample["page_base"] + pt[seq, (step % steps_per_seq) * pp + c], 0, 0)
        return index

    def const_map(*args):
        return (0, 0)

    seq_spec = pl.BlockSpec((None, n_heads, D_V), seq_map)
    page_specs = [pl.BlockSpec((None, page_rows, D_V), page_map(c)) for c in range(pp)]
    vec = pl.BlockSpec((1, D_SUB), const_map)
    out, attn = pl.pallas_call(
        functools.partial(_mlp_paged_kernel, host=host, grid_dims=grid, pp=pp,
                          steps_per_seq=steps_per_seq, n_heads=n_heads, lam_init=sample["lam_init"]),
        grid_spec=pltpu.PrefetchScalarGridSpec(
            num_scalar_prefetch=1,
            grid=grid,
            in_specs=host_specs + [seq_spec, seq_spec, seq_spec] + page_specs + page_specs
                     + [vec, vec, vec, vec, pl.BlockSpec((1, D_V), const_map)],
            out_specs=[host_out,
                       pl.BlockSpec((None, n_heads, D_V),
                                    lambda *args: (linear(args[:-1]) // steps_per_seq, 0, 0))],
            scratch_shapes=[pltpu.VMEM((rows, D_V), BF16),
                            pltpu.VMEM((rows, page_rows), F32),
                            pltpu.VMEM((rows, 1), F32), pltpu.VMEM((rows, 1), F32),
                            pltpu.VMEM((rows, D_V), F32)]),
        out_shape=[jax.ShapeDtypeStruct((m, n), out_dtype),
                   jax.ShapeDtypeStruct((n_seq, n_heads, D_V), BF16)],
        compiler_params=_params(*(["arbitrary"] * len(grid))),
        name="mlp_" + host + "_paged_attention",
    )(page_table, *host_args, sample["q"], sample["k_new"], sample["v_new"],
      *([sample["cache_k"]] * pp), *([sample["cache_v"]] * pp), *sample["lam_vecs"],
      sample["subln_g"].reshape(1, D_V))
    return out, attn.reshape(n_seq, n_heads * D_V)


HALO = POOL_BUF + 1


def _pool_prompt_kernel(u_ref, halo_ref, w_ref, sc_ref, o_ref, ext_ref, *, tm, group):
    i = pl.program_id(0)
    halo = halo_ref[...]
    ext_ref[0:HALO, :] = jnp.where(i > 0, halo, jnp.zeros_like(halo))
    ext_ref[HALO:, :] = u_ref[...]
    pos = i * tm + lax.broadcasted_iota(jnp.int32, (tm, 1), 0)
    for g, win in enumerate(POOL_WINDOWS):
        cols = slice(g * group, (g + 1) * group)
        cur = u_ref[:, cols]
        tot = cur
        for k in range(1, win):
            tot = tot + ext_ref[HALO - k:HALO - k + tm, cols]
        cnt = jnp.minimum(win, pos + 1).astype(F32)
        d = tot / cnt - cur
        y = jnp.dot(d.astype(BF16), w_ref[g], preferred_element_type=F32) * sc_ref[:, cols]
        o_ref[:, cols] = y.astype(o_ref.dtype)


def pool_mix_prompt(u, pool_w, pool_scale, *, tm):
    s, width = u.shape
    group = width // len(POOL_WINDOWS)
    ratio = tm // HALO
    return pl.pallas_call(
        functools.partial(_pool_prompt_kernel, tm=tm, group=group),
        grid=(s // tm,),
        in_specs=[pl.BlockSpec((tm, width), lambda i: (i, 0)),
                  pl.BlockSpec((HALO, width), lambda i: (jnp.maximum(i * ratio - 1, 0), 0)),
                  pl.BlockSpec(pool_w.shape, lambda i: (0, 0, 0)),
                  pl.BlockSpec((1, width), lambda i: (0, 0))],
        out_specs=pl.BlockSpec((tm, width), lambda i: (i, 0)),
        out_shape=jax.ShapeDtypeStruct((s, width), BF16),
        scratch_shapes=[pltpu.VMEM((tm + HALO, width), F32)],
        compiler_params=_params("arbitrary"),
        name="pool_mix_prompt",
    )(u, u, pool_w, pool_scale.reshape(1, width))


def _pool_sample_kernel(st_ref, us_ref, w_ref, sc_ref, o_ref, *, group, first_pos):
    cur_all = us_ref[...]
    for g, win in enumerate(POOL_WINDOWS):
        cols = slice(g * group, (g + 1) * group)
        cur = cur_all[:, cols]
        tot = cur
        for k in range(1, win):
            tot = tot + st_ref[POOL_BUF - k, :, cols]
        cnt = float(min(win, first_pos + 1))
        d = tot / cnt - cur
        y = jnp.dot(d.astype(BF16), w_ref[g], preferred_element_type=F32) * sc_ref[:, cols]
        o_ref[:, cols] = y.astype(o_ref.dtype)


def pool_mix_sample(state_t, us, pool_w, pool_scale, first_pos):
    b, width = us.shape
    group = width // len(POOL_WINDOWS)
    return pl.pallas_call(
        functools.partial(_pool_sample_kernel, group=group, first_pos=first_pos),
        grid=(1,),
        in_specs=[pl.BlockSpec(state_t.shape, lambda i: (0, 0, 0)),
                  pl.BlockSpec((b, width), lambda i: (0, 0)),
                  pl.BlockSpec(pool_w.shape, lambda i: (0, 0, 0)),
                  pl.BlockSpec((1, width), lambda i: (0, 0))],
        out_specs=pl.BlockSpec((b, width), lambda i: (0, 0)),
        out_shape=jax.ShapeDtypeStruct((b, width), BF16),
        compiler_params=_params("arbitrary"),
        name="pool_mix_sample",
    )(state_t, us, pool_w, pool_scale.reshape(1, width))


def _block_diag_ones():
    r = jnp.arange(MXU_DIM) // D_SUB
    return (r[:, None] == r[None, :]).astype(BF16)


def _tile_lanes(vec, n):
    return jnp.tile(vec, n // vec.shape[0]).reshape(1, n)


def project(x, pos, wts, *, tm, tn):
    d = x.shape[1]
    attn_w = d // 2
    h = rmsnorm(x, wts["norm1_g"], min(tm, 256))
    cos, s1, s2 = rope_tables(pos, tm)
    tab_spec = pl.BlockSpec((tm, LANES), lambda n, i: (i, 0))
    vec_spec = pl.BlockSpec((1, tn), lambda n, i: (0, 0))
    bd_spec = pl.BlockSpec((MXU_DIM, MXU_DIM), lambda n, i: (0, 0))
    specs = (vec_spec, bd_spec, tab_spec, tab_spec, tab_spec)
    bd = _block_diag_ones()
    q = mm_fullk([h], wts["w_in"], 0, attn_w, tm=tm, tn=tn, out_dtype=BF16, epilogue="qk",
                 extra=(_tile_lanes(wts["q_norm_g"], tn), bd, cos, s1, s2), extra_specs=specs,
                 scale=D_SUB ** -0.5, name="proj_q")
    k = mm_fullk([h], wts["w_in"], attn_w, attn_w, tm=tm, tn=tn, out_dtype=F32, epilogue="qk",
                 extra=(_tile_lanes(wts["k_norm_g"], tn), bd, cos, s1, s2), extra_specs=specs,
                 name="proj_k")
    v = mm_fullk([h], wts["w_in"], 2 * attn_w, attn_w, tm=tm, tn=tn, out_dtype=F32, name="proj_v")
    u = mm_fullk([h], wts["w_in"], 3 * attn_w, d - attn_w, tm=tm, tn=tn, out_dtype=F32, name="proj_u")
    return q, k, v, u


def finish(x, attn, pool, wts, *, tm, tn, tn_up, tm_down, tn_down, tk_down, sample=None,
           pages_per_step=None):
    d = x.shape[1]
    x1 = mm_fullk([attn, pool], wts["w_out"], 0, d, tm=tm, tn=tn, out_dtype=F32, epilogue="resid",
                  extra=(x,), extra_specs=(pl.BlockSpec((tm, tn), lambda n, i: (i, n)),),
                  name="out_proj")
    h2 = rmsnorm(x1, wts["norm2_g"], min(tm, 256))
    if sample is None:
        a = mm_fullk([h2], wts["w_up"], 0, wts["w_up"].shape[1], tm=tm, tn=tn_up, out_dtype=BF16,
                     epilogue="relu2", name="mlp_up")
        y = mm_ktiled_resid(a, wts["w_down"], x1, tm=tm_down, tn=tn_down, tk=tk_down, name="mlp_down")
        return y, None
    n_seq = sample["q"].shape[0]
    n_lo = n_seq // 2
    a, attn_lo = mlp_matmul_with_paged_attention(
        "up", h2, wts["w_up"], None, sample, 0, n_lo, tm=tm, tn=tn_up, tk=d,
        pages_per_step=pages_per_step)
    y, attn_hi = mlp_matmul_with_paged_attention(
        "down", a, wts["w_down"], x1, sample, n_lo, n_seq - n_lo, tm=tm_down, tn=tn_down, tk=tk_down,
        pages_per_step=pages_per_step)
    return y, jnp.concatenate([attn_lo, attn_hi], axis=0)


def kernel(x_prompt, x_sample, cache_k, cache_v, state_pool, page_table, norm1_g, w_in, q_norm_g,
           k_norm_g, lambda_q1, lambda_k1, lambda_q2, lambda_k2, subln_g, pool_w, pool_scale, w_out,
           norm2_g, w_up, w_down):
    batch, seq, d_model = x_prompt.shape
    dec_batch, dec_seq, _ = x_sample.shape
    assert batch == 1 and dec_seq == 1
    depth = w_in.shape[0]
    n_phys = cache_k.shape[1]
    n_heads = cache_k.shape[3]
    attn_w = n_heads * D_V
    pool_width = d_model - attn_w
    past_len = page_table.shape[1] * PAGE_SIZE

    xp = x_prompt.reshape(seq, d_model)
    xs = x_sample.reshape(dec_batch, d_model)
    pos_p = jnp.arange(seq, dtype=F32).reshape(seq, 1)
    pos_s = jnp.full((dec_batch, 1), past_len, F32)

    outs = [[] for _ in range(6)]
    for l in range(depth):
        lam_init = 0.8 - 0.6 * math.exp(-0.3 * l)
        wts = {
            "norm1_g": norm1_g[l], "q_norm_g": q_norm_g[l], "k_norm_g": k_norm_g[l],
            "norm2_g": norm2_g[l],
            "w_in": w_in[l].astype(BF16), "w_out": w_out[l].astype(BF16),
            "w_up": w_up[l].astype(BF16), "w_down": w_down[l].astype(BF16),
        }
        pool_w_l = pool_w[l].astype(BF16)
        lam_vecs = [v[l].reshape(1, D_SUB) for v in (lambda_q1, lambda_k1, lambda_q2, lambda_k2)]

        q, k, v, u = project(xp, pos_p, wts, tm=512, tn=1024)
        qs, kn, vn, us = project(xs, pos_s, wts, tm=dec_batch, tn=512)
        to_heads = lambda a: a.reshape(dec_batch, n_heads, D_V)
        sample = {
            "n_heads": n_heads, "page_table": page_table, "page_base": l * n_phys,
            "q": to_heads(qs), "k_new": to_heads(kn), "v_new": to_heads(vn),
            "cache_k": cache_k.reshape(-1, PAGE_SIZE * n_heads, D_V),
            "cache_v": cache_v.reshape(-1, PAGE_SIZE * n_heads, D_V),
            "lam_vecs": lam_vecs, "subln_g": subln_g[l], "lam_init": lam_init,
        }

        attn = flash_diff_attention(q, k, v, lam_vecs, subln_g[l], lam_init, tq=512, tk=512)
        pool = pool_mix_prompt(u, pool_w_l, pool_scale[l], tm=256)
        xp, attn_s = finish(xp, attn, pool, wts, tm=512, tn=1024, tn_up=1024, tm_down=1024,
                            tn_down=1024, tk_down=2048, sample=sample, pages_per_step=4)
        outs[0].append(k.reshape(batch, seq, n_heads, D_V))
        outs[1].append(v.reshape(batch, seq, n_heads, D_V))
        outs[2].append(u[seq - POOL_BUF:].reshape(batch, POOL_BUF, pool_width))

        state_l = state_pool[l]
        pool_s = pool_mix_sample(jnp.swapaxes(state_l, 0, 1), us, pool_w_l, pool_scale[l], past_len)
        xs, _ = finish(xs, attn_s, pool_s, wts, tm=dec_batch, tn=1024, tn_up=1024, tm_down=dec_batch,
                       tn_down=2048, tk_down=2048)
        outs[3].append(kn.reshape(dec_batch, dec_seq, n_heads, D_V))
        outs[4].append(vn.reshape(dec_batch, dec_seq, n_heads, D_V))
        outs[5].append(jnp.concatenate([state_l[:, 1:], us[:, None, :]], axis=1))

    k_prompt, v_prompt, pool_prompt, k_sample, v_sample, pool_sample = (jnp.stack(o) for o in outs)
    return (xp.reshape(batch, seq, d_model), xs.reshape(dec_batch, dec_seq, d_model),
            k_prompt, v_prompt, pool_prompt, k_sample, v_sample, pool_sample)
```

```python
import functools
import math

import jax
import jax.numpy as jnp
from jax import lax
from jax.experimental import pallas as pl
from jax.experimental.pallas import tpu as pltpu

PAGE_SIZE = 128
D_V = 128
D_SUB = D_V // 2
ROT_DIM = D_SUB // 4
ROPE_THETA = 500000.0
POOL_WINDOWS = (2, 4, 8, 16)
POOL_BUF = max(POOL_WINDOWS) - 1
EPS = 1e-6
NEG_INF = -1e30

LANES = 128
MXU_DIM = 256
ONES_ROWS = 16
VMEM_LIMIT_BYTES = 56 * 1024 * 1024

F32 = jnp.float32
BF16 = jnp.bfloat16


def _params(*sem):
    return pltpu.CompilerParams(dimension_semantics=sem, vmem_limit_bytes=VMEM_LIMIT_BYTES)


def _rmsnorm_kernel(x_ref, g_ref, o_ref):
    x = x_ref[...]
    ms = jnp.mean(x * x, axis=-1, keepdims=True)
    o_ref[...] = (x * lax.rsqrt(ms + EPS) * g_ref[...]).astype(o_ref.dtype)


def rmsnorm(x, g, tm):
    m, d = x.shape
    return pl.pallas_call(
        _rmsnorm_kernel,
        grid=(m // tm,),
        in_specs=[pl.BlockSpec((tm, d), lambda i: (i, 0)),
                  pl.BlockSpec((1, d), lambda i: (0, 0))],
        out_specs=pl.BlockSpec((tm, d), lambda i: (i, 0)),
        out_shape=jax.ShapeDtypeStruct((m, d), BF16),
        compiler_params=_params("arbitrary"),
        name="rmsnorm",
    )(x, g.reshape(1, d))


def _rope_table_kernel(pos_ref, invf_ref, c_ref, s1_ref, s2_ref):
    ang = pos_ref[...] * invf_ref[...]
    c = jnp.cos(ang)
    s = jnp.sin(ang)
    r = lax.broadcasted_iota(jnp.int32, ang.shape, 1) & (D_SUB - 1)
    c_ref[...] = c
    s1_ref[...] = jnp.where(r < ROT_DIM // 2, -s, 0.0)
    s2_ref[...] = jnp.where((r >= ROT_DIM // 2) & (r < ROT_DIM), s, 0.0)


def rope_tables(pos, tm):
    m = pos.shape[0]
    inv_freq = ROPE_THETA ** (-jnp.arange(0, ROT_DIM, 2, dtype=F32) / ROT_DIM)
    sub = jnp.concatenate([inv_freq, inv_freq, jnp.zeros((D_SUB - ROT_DIM,), F32)])
    invf_lane = jnp.tile(sub, LANES // D_SUB).reshape(1, LANES)
    spec = pl.BlockSpec((tm, LANES), lambda i: (i, 0))
    shp = jax.ShapeDtypeStruct((m, LANES), F32)
    return pl.pallas_call(
        _rope_table_kernel,
        grid=(m // tm,),
        in_specs=[pl.BlockSpec((tm, 1), lambda i: (i, 0)),
                  pl.BlockSpec((1, LANES), lambda i: (0, 0))],
        out_specs=[spec, spec, spec],
        out_shape=[shp, shp, shp],
        compiler_params=_params("arbitrary"),
        name="rope_tables",
    )(pos, invf_lane)


def _qk_norm_rope(z, g_ref, bd_ref, c_ref, s1_ref, s2_ref, scale):
    tn = z.shape[1]
    zz = z * z
    hi = zz.astype(BF16)
    lo = (zz - hi.astype(F32)).astype(BF16)
    bd = bd_ref[...]
    parts = []
    for c in range(tn // MXU_DIM):
        sl = slice(c * MXU_DIM, (c + 1) * MXU_DIM)
        parts.append(jnp.dot(hi[:, sl], bd, preferred_element_type=F32)
                     + jnp.dot(lo[:, sl], bd, preferred_element_type=F32))
    ss = jnp.concatenate(parts, axis=1) if len(parts) > 1 else parts[0]
    y = z * lax.rsqrt(ss * (1.0 / D_SUB) + EPS) * g_ref[...]
    cos, s1, s2 = c_ref[...], s1_ref[...], s2_ref[...]
    outs = []
    for c in range(tn // LANES):
        ys = y[:, c * LANES:(c + 1) * LANES]
        up = pltpu.roll(ys, LANES - ROT_DIM // 2, axis=1)
        dn = pltpu.roll(ys, ROT_DIM // 2, axis=1)
        outs.append(ys * cos + up * s1 + dn * s2)
    out = jnp.concatenate(outs, axis=1) if len(outs) > 1 else outs[0]
    if scale != 1.0:
        out = out * scale
    return out


def _mm_fullk_kernel(*refs, n_lhs, epilogue, scale):
    x_refs = refs[:n_lhs]
    w_ref = refs[n_lhs]
    extra = refs[n_lhs + 1:-1]
    o_ref = refs[-1]
    kk = w_ref.shape[0] // n_lhs
    acc = None
    for i, x_ref in enumerate(x_refs):
        part = jnp.dot(x_ref[...], w_ref[i * kk:(i + 1) * kk, :], preferred_element_type=F32)
        acc = part if acc is None else acc + part
    if epilogue == "relu2":
        a = jnp.maximum(acc, 0.0)
        acc = a * a
    elif epilogue == "resid":
        acc = extra[0][...] + acc
    elif epilogue == "qk":
        acc = _qk_norm_rope(acc, *extra, scale)
    o_ref[...] = acc.astype(o_ref.dtype)


def mm_fullk(xs, w, col_off, n_cols, *, tm, tn, out_dtype, epilogue=None, extra=(), extra_specs=(),
             scale=1.0, name="mm"):
    m = xs[0].shape[0]
    k_total = w.shape[0]
    off = col_off // tn
    x_specs = [pl.BlockSpec((tm, x.shape[1]), lambda n, i: (i, 0)) for x in xs]
    w_spec = pl.BlockSpec((k_total, tn), lambda n, i: (0, n + off))
    return pl.pallas_call(
        functools.partial(_mm_fullk_kernel, n_lhs=len(xs), epilogue=epilogue, scale=scale),
        grid=(n_cols // tn, m // tm),
        in_specs=x_specs + [w_spec] + list(extra_specs),
        out_specs=pl.BlockSpec((tm, tn), lambda n, i: (i, n)),
        out_shape=jax.ShapeDtypeStruct((m, n_cols), out_dtype),
        compiler_params=_params("arbitrary", "arbitrary"),
        name=name,
    )(*xs, w, *extra)


def _mm_ktiled_kernel(x_ref, w_ref, r_ref, o_ref):
    part = jnp.dot(x_ref[...], w_ref[...], preferred_element_type=F32)

    @pl.when(pl.program_id(2) == 0)
    def _():
        o_ref[...] = r_ref[...] + part

    @pl.when(pl.program_id(2) > 0)
    def _():
        o_ref[...] += part


def mm_ktiled_resid(x, w, resid, *, tm, tn, tk, name="mm_ktiled"):
    m, kdim = x.shape
    n = w.shape[1]
    nk = kdim // tk
    return pl.pallas_call(
        _mm_ktiled_kernel,
        grid=(m // tm, n // tn, nk),
        in_specs=[pl.BlockSpec((tm, tk), lambda i, j, k: (i, k)),
                  pl.BlockSpec((tk, tn), lambda i, j, k: (k, j)),
                  pl.BlockSpec((tm, tn), lambda i, j, k: (i, j))],
        out_specs=pl.BlockSpec((tm, tn), lambda i, j, k: (i, j)),
        out_shape=jax.ShapeDtypeStruct((m, n), F32),
        compiler_params=_params("arbitrary", "arbitrary", "arbitrary"),
        name=name,
    )(x, w, resid)


def _diff_lambda(lq1_ref, lk1_ref, lq2_ref, lk2_ref, lam_init):
    a = jnp.sum(lq1_ref[...] * lk1_ref[...], axis=-1, keepdims=True)
    b = jnp.sum(lq2_ref[...] * lk2_ref[...], axis=-1, keepdims=True)
    return jnp.exp(a) - jnp.exp(b) + lam_init


def _flash_kernel(q_ref, k_ref, v_ref, lq1_ref, lk1_ref, lq2_ref, lk2_ref, sg_ref, *rest,
                  n_cast, tq, tk, lam_init):
    cast_in, o_ref, cast_out = rest[:n_cast], rest[n_cast], rest[n_cast + 1:2 * n_cast + 1]
    kb_ref, vt_ref, m_ref, acc_ref, st_ref = rest[2 * n_cast + 1:]
    qi = pl.program_id(1)
    n_chunks = vt_ref.shape[0]

    for w_ref, wb_ref in zip(cast_in, cast_out):
        wb_ref[...] = w_ref[...].astype(BF16)

    @pl.when(qi == 0)
    def _():
        kb_ref[...] = k_ref[...].astype(BF16)
        vt_ref[:, D_V:, :] = jnp.ones((n_chunks, ONES_ROWS, tk), BF16)
        for c in range(n_chunks):
            vt_ref[c, :D_V, :] = v_ref[c * tk:(c + 1) * tk, :].T.astype(BF16)

    q = q_ref[...].astype(F32)
    lane = lax.broadcasted_iota(jnp.int32, q.shape, 1)
    q2 = jnp.concatenate([jnp.where(lane < D_SUB, q, 0.0),
                          jnp.where(lane >= D_SUB, q, 0.0)], axis=0).astype(BF16)

    m_ref[...] = jnp.full_like(m_ref, NEG_INF)
    acc_ref[...] = jnp.zeros_like(acc_ref)

    def scores(j, slot):
        start = pl.multiple_of(j * tk, tk)
        kc = kb_ref[pl.ds(start, tk), :]
        st_ref[slot] = lax.dot_general(kc, q2, (((1,), (1,)), ((), ())),
                                       preferred_element_type=F32)

    def consume(j, slot, masked):
        st = st_ref[slot]
        if masked:
            key = lax.broadcasted_iota(jnp.int32, st.shape, 0) + j * tk
            qpos = (lax.broadcasted_iota(jnp.int32, st.shape, 1) & (tq - 1)) + qi * tq
            st = jnp.where(key <= qpos, st, NEG_INF)
        m_prev = m_ref[...]
        m_new = jnp.maximum(m_prev, jnp.max(st, axis=0, keepdims=True))
        alpha = jnp.exp(m_prev - m_new)
        pt = jnp.exp(st - m_new)
        pv = jnp.dot(vt_ref[j], pt.astype(BF16), preferred_element_type=F32)
        acc_ref[...] = alpha * acc_ref[...] + pv
        m_ref[...] = m_new

    n_full = (qi * tq) // tk
    scores(0, 0)

    def body(jj, carry):
        scores(2 * jj + 1, 1)
        consume(2 * jj, 0, False)
        scores(2 * jj + 2, 0)
        consume(2 * jj + 1, 1, False)
        return carry

    lax.fori_loop(0, n_full // 2, body, 0)

    @pl.when(n_full % 2 == 1)
    def _():
        scores(n_full, 1)
        consume(n_full - 1, 0, False)
        consume(n_full, 1, True)

    @pl.when(n_full % 2 == 0)
    def _():
        consume(n_full, 0, True)

    acc = acc_ref[:D_V, :]
    l = acc_ref[D_V:D_V + 1, :]
    lam = _diff_lambda(lq1_ref, lk1_ref, lq2_ref, lk2_ref, lam_init)
    o = acc[:, :tq] / l[:, :tq] - lam * (acc[:, tq:] / l[:, tq:])
    ms = jnp.mean(o * o, axis=0, keepdims=True)
    y = o * lax.rsqrt(ms + EPS) * sg_ref[...] * (1.0 - lam_init)
    o_ref[...] = y.T.astype(o_ref.dtype)


def flash_diff_attention(q, k, v, lam_vecs, subln_g, lam_init, cast_weights, *, tq, tk):
    s, width = q.shape
    n_heads = width // D_V
    n_q = s // tq
    assert tk % tq == 0 and s % tk == 0
    vec = pl.BlockSpec((1, D_SUB), lambda h, i: (0, 0))
    cast_specs = []
    for w in cast_weights:
        assert w.shape[0] % n_heads == 0 and w.shape[1] % n_q == 0
        cast_specs.append(pl.BlockSpec((w.shape[0] // n_heads, w.shape[1] // n_q), lambda h, i: (h, i)))
    outs = pl.pallas_call(
        functools.partial(_flash_kernel, n_cast=len(cast_weights), tq=tq, tk=tk, lam_init=lam_init),
        grid=(n_heads, n_q),
        in_specs=[pl.BlockSpec((tq, D_V), lambda h, i: (i, h)),
                  pl.BlockSpec((s, D_V), lambda h, i: (0, h)),
                  pl.BlockSpec((s, D_V), lambda h, i: (0, h)),
                  vec, vec, vec, vec,
                  pl.BlockSpec((D_V, 1), lambda h, i: (0, 0))] + cast_specs,
        out_specs=[pl.BlockSpec((tq, D_V), lambda h, i: (i, h))] + cast_specs,
        out_shape=[jax.ShapeDtypeStruct((s, width), BF16)]
                  + [jax.ShapeDtypeStruct(w.shape, BF16) for w in cast_weights],
        scratch_shapes=[pltpu.VMEM((s, D_V), BF16),
                        pltpu.VMEM((s // tk, D_V + ONES_ROWS, tk), BF16),
                        pltpu.VMEM((1, 2 * tq), F32),
                        pltpu.VMEM((D_V + ONES_ROWS, 2 * tq), F32),
                        pltpu.VMEM((2, tk, 2 * tq), F32)],
        compiler_params=_params("arbitrary", "arbitrary"),
        name="flash_diff_attention",
    )(q, k, v, *lam_vecs, subln_g.reshape(D_V, 1), *cast_weights)
    return outs[0], outs[1:]


def _paged_update(p_idx, n_steps, q_ref, kn_ref, vn_ref, k_refs, v_refs, lam_refs, sg_ref, o_ref,
                  scratch, host_work, *, n_heads, lam_init):
    lq1_ref, lk1_ref, lq2_ref, lk2_ref = lam_refs
    q2_ref, bias_ref, m_ref, l_ref, acc_ref = scratch

    @pl.when(p_idx == 0)
    def _():
        q = q_ref[...].astype(F32)
        lane = lax.broadcasted_iota(jnp.int32, q.shape, 1)
        q2_ref[...] = jnp.concatenate([jnp.where(lane < D_SUB, q, 0.0),
                                       jnp.where(lane >= D_SUB, q, 0.0)], axis=0).astype(BF16)
        row = lax.broadcasted_iota(jnp.int32, bias_ref.shape, 0)
        col = lax.broadcasted_iota(jnp.int32, bias_ref.shape, 1)
        same_head = (col & (n_heads - 1)) == (row & (n_heads - 1))
        bias_ref[...] = jnp.where(same_head, 0.0, NEG_INF)
        m_ref[...] = jnp.full_like(m_ref, NEG_INF)
        l_ref[...] = jnp.zeros_like(l_ref)
        acc_ref[...] = jnp.zeros_like(acc_ref)

    q2 = q2_ref[...]
    bias = bias_ref[...]
    s_parts = [lax.dot_general(q2, k_ref[...].astype(BF16), (((1,), (1,)), ((), ())),
                               preferred_element_type=F32) + bias for k_ref in k_refs]
    host_work()
    m_prev = m_ref[...]
    m_new = m_prev
    for s in s_parts:
        m_new = jnp.maximum(m_new, jnp.max(s, axis=1, keepdims=True))
    alpha = jnp.exp(m_prev - m_new)
    l_new = alpha * l_ref[...]
    acc = alpha * acc_ref[...]
    for s, v_ref in zip(s_parts, v_refs):
        p = jnp.exp(s - m_new)
        l_new = l_new + jnp.sum(p, axis=1, keepdims=True)
        acc = acc + jnp.dot(p.astype(BF16), v_ref[...].astype(BF16), preferred_element_type=F32)
    l_ref[...] = l_new
    acc_ref[...] = acc
    m_ref[...] = m_new

    @pl.when(p_idx == n_steps - 1)
    def _():
        kn2 = jnp.concatenate([kn_ref[...], kn_ref[...]], axis=0)
        vn2 = jnp.concatenate([vn_ref[...], vn_ref[...]], axis=0)
        s_new = jnp.sum(q2_ref[...].astype(F32) * kn2, axis=1, keepdims=True)
        m_old = m_ref[...]
        m_fin = jnp.maximum(m_old, s_new)
        a = jnp.exp(m_old - m_fin)
        p_new = jnp.exp(s_new - m_fin)
        l = a * l_ref[...] + p_new
        acc_f = a * acc_ref[...] + p_new * vn2
        lam = _diff_lambda(lq1_ref, lk1_ref, lq2_ref, lk2_ref, lam_init)
        o = acc_f[:n_heads] / l[:n_heads] - lam * (acc_f[n_heads:] / l[n_heads:])
        ms = jnp.mean(o * o, axis=1, keepdims=True)
        y = o * lax.rsqrt(ms + EPS) * sg_ref[...] * (1.0 - lam_init)
        o_ref[...] = y.astype(o_ref.dtype)


def _mlp_paged_kernel(pt_ref, *refs, host, grid_dims, pp, steps_per_seq, n_heads, lam_init):
    del pt_ref
    n_host = 2 if host == "up" else 3
    host_in, rest = refs[:n_host], refs[n_host:]
    q_ref, kn_ref, vn_ref = rest[:3]
    k_refs, v_refs = rest[3:3 + pp], rest[3 + pp:3 + 2 * pp]
    lam_refs = rest[3 + 2 * pp:7 + 2 * pp]
    sg_ref, o_ref, attn_ref = rest[7 + 2 * pp:10 + 2 * pp]
    scratch = rest[10 + 2 * pp:]

    parts = []

    def host_work():
        part = jnp.dot(host_in[0][...], host_in[1][...], preferred_element_type=F32)
        if host == "up":
            a = jnp.maximum(part, 0.0)
            o_ref[...] = (a * a).astype(o_ref.dtype)
        else:
            parts.append(part)

    step = pl.program_id(0)
    for axis in range(1, len(grid_dims)):
        step = step * grid_dims[axis] + pl.program_id(axis)
    _paged_update(lax.rem(step, steps_per_seq), steps_per_seq, q_ref, kn_ref, vn_ref, k_refs, v_refs,
                  lam_refs, sg_ref, attn_ref, scratch, host_work, n_heads=n_heads, lam_init=lam_init)

    if host == "down":
        @pl.when(pl.program_id(2) == 0)
        def _():
            o_ref[...] = host_in[2][...] + parts[0]

        @pl.when(pl.program_id(2) > 0)
        def _():
            o_ref[...] += parts[0]


def mlp_matmul_with_paged_attention(host, x, w, resid, sample, seq_lo, n_seq, *, tm, tn, tk,
                                    pages_per_step):
    m, kdim = x.shape
    n = w.shape[1]
    pp = pages_per_step
    n_heads = sample["n_heads"]
    page_table = sample["page_table"]
    steps_per_seq = page_table.shape[1] // pp
    rows = 2 * n_heads
    page_rows = PAGE_SIZE * n_heads
    if host == "up":
        grid = (n // tn, m // tm)
        assert tk == kdim
        host_specs = [pl.BlockSpec((tm, kdim), lambda j, i, pt: (i, 0)),
                      pl.BlockSpec((kdim, tn), lambda j, i, pt: (0, j))]
        host_out = pl.BlockSpec((tm, tn), lambda j, i, pt: (i, j))
        host_args, out_dtype = (x, w), BF16
    else:
        grid = (m // tm, n // tn, kdim // tk)
        host_specs = [pl.BlockSpec((tm, tk), lambda i, j, k, pt: (i, k)),
                      pl.BlockSpec((tk, tn), lambda i, j, k, pt: (k, j)),
                      pl.BlockSpec((tm, tn), lambda i, j, k, pt: (i, j))]
        host_out = pl.BlockSpec((tm, tn), lambda i, j, k, pt: (i, j))
        host_args, out_dtype = (x, w, resid), F32
    n_steps = math.prod(grid)
    assert n_steps == n_seq * steps_per_seq, (grid, n_seq, steps_per_seq)

    def linear(ids):
        step = ids[0]
        for axis in range(1, len(grid)):
            step = step * grid[axis] + ids[axis]
        return step

    def seq_map(*args):
        return (seq_lo + linear(args[:-1]) // steps_per_seq, 0, 0)

    def page_map(c):
        def index(*args):
            step, pt = linear(args[:-1]), args[-1]
            seq = seq_lo + step // steps_per_seq
            return (sample["page_base"] + pt[seq, (step % steps_per_seq) * pp + c], 0, 0)
        return index

    def const_map(*args):
        return (0, 0)

    seq_spec = pl.BlockSpec((None, n_heads, D_V), seq_map)
    page_specs = [pl.BlockSpec((None, page_rows, D_V), page_map(c)) for c in range(pp)]
    vec = pl.BlockSpec((1, D_SUB), const_map)
    out, attn = pl.pallas_call(
        functools.partial(_mlp_paged_kernel, host=host, grid_dims=grid, pp=pp,
                          steps_per_seq=steps_per_seq, n_heads=n_heads, lam_init=sample["lam_init"]),
        grid_spec=pltpu.PrefetchScalarGridSpec(
            num_scalar_prefetch=1,
            grid=grid,
            in_specs=host_specs + [seq_spec, seq_spec, seq_spec] + page_specs + page_specs
                     + [vec, vec, vec, vec, pl.BlockSpec((1, D_V), const_map)],
            out_specs=[host_out,
                       pl.BlockSpec((None, n_heads, D_V),
                                    lambda *args: (linear(args[:-1]) // steps_per_seq, 0, 0))],
            scratch_shapes=[pltpu.VMEM((rows, D_V), BF16),
                            pltpu.VMEM((rows, page_rows), F32),
                            pltpu.VMEM((rows, 1), F32), pltpu.VMEM((rows, 1), F32),
                            pltpu.VMEM((rows, D_V), F32)]),
        out_shape=[jax.ShapeDtypeStruct((m, n), out_dtype),
                   jax.ShapeDtypeStruct((n_seq, n_heads, D_V), BF16)],
        compiler_params=_params(*(["arbitrary"] * len(grid))),
        name="mlp_" + host + "_paged_attention",
    )(page_table, *host_args, sample["q"], sample["k_new"], sample["v_new"],
      *([sample["cache_k"]] * pp), *([sample["cache_v"]] * pp), *sample["lam_vecs"],
      sample["subln_g"].reshape(1, D_V))
    return out, attn.reshape(n_seq, n_heads * D_V)


HALO = POOL_BUF + 1


def _pool_prompt_kernel(u_ref, halo_ref, w_ref, sc_ref, o_ref, ext_ref, *, tm, group):
    i = pl.program_id(0)
    halo = halo_ref[...]
    ext_ref[0:HALO, :] = jnp.where(i > 0, halo, jnp.zeros_like(halo))
    ext_ref[HALO:, :] = u_ref[...]
    pos = i * tm + lax.broadcasted_iota(jnp.int32, (tm, 1), 0)
    for g, win in enumerate(POOL_WINDOWS):
        cols = slice(g * group, (g + 1) * group)
        cur = u_ref[:, cols]
        tot = cur
        for k in range(1, win):
            tot = tot + ext_ref[HALO - k:HALO - k + tm, cols]
        cnt = jnp.minimum(win, pos + 1).astype(F32)
        d = tot / cnt - cur
        y = jnp.dot(d.astype(BF16), w_ref[g], preferred_element_type=F32) * sc_ref[:, cols]
        o_ref[:, cols] = y.astype(o_ref.dtype)


def pool_mix_prompt(u, pool_w, pool_scale, *, tm):
    s, width = u.shape
    group = width // len(POOL_WINDOWS)
    ratio = tm // HALO
    return pl.pallas_call(
        functools.partial(_pool_prompt_kernel, tm=tm, group=group),
        grid=(s // tm,),
        in_specs=[pl.BlockSpec((tm, width), lambda i: (i, 0)),
                  pl.BlockSpec((HALO, width), lambda i: (jnp.maximum(i * ratio - 1, 0), 0)),
                  pl.BlockSpec(pool_w.shape, lambda i: (0, 0, 0)),
                  pl.BlockSpec((1, width), lambda i: (0, 0))],
        out_specs=pl.BlockSpec((tm, width), lambda i: (i, 0)),
        out_shape=jax.ShapeDtypeStruct((s, width), BF16),
        scratch_shapes=[pltpu.VMEM((tm + HALO, width), F32)],
        compiler_params=_params("arbitrary"),
        name="pool_mix_prompt",
    )(u, u, pool_w, pool_scale.reshape(1, width))


def _pool_sample_kernel(st_ref, us_ref, w_ref, sc_ref, o_ref, *, group, first_pos):
    cur_all = us_ref[...]
    for g, win in enumerate(POOL_WINDOWS):
        cols = slice(g * group, (g + 1) * group)
        cur = cur_all[:, cols]
        tot = cur
        for k in range(1, win):
            tot = tot + st_ref[POOL_BUF - k, :, cols]
        cnt = float(min(win, first_pos + 1))
        d = tot / cnt - cur
        y = jnp.dot(d.astype(BF16), w_ref[g], preferred_element_type=F32) * sc_ref[:, cols]
        o_ref[:, cols] = y.astype(o_ref.dtype)


def pool_mix_sample(state_t, us, pool_w, pool_scale, first_pos):
    b, width = us.shape
    group = width // len(POOL_WINDOWS)
    return pl.pallas_call(
        functools.partial(_pool_sample_kernel, group=group, first_pos=first_pos),
        grid=(1,),
        in_specs=[pl.BlockSpec(state_t.shape, lambda i: (0, 0, 0)),
                  pl.BlockSpec((b, width), lambda i: (0, 0)),
                  pl.BlockSpec(pool_w.shape, lambda i: (0, 0, 0)),
                  pl.BlockSpec((1, width), lambda i: (0, 0))],
        out_specs=pl.BlockSpec((b, width), lambda i: (0, 0)),
        out_shape=jax.ShapeDtypeStruct((b, width), BF16),
        compiler_params=_params("arbitrary"),
        name="pool_mix_sample",
    )(state_t, us, pool_w, pool_scale.reshape(1, width))


def _block_diag_ones():
    r = jnp.arange(MXU_DIM) // D_SUB
    return (r[:, None] == r[None, :]).astype(BF16)


def _tile_lanes(vec, n):
    return jnp.tile(vec, n // vec.shape[0]).reshape(1, n)


def project(x, pos, wts, *, tm, tn):
    d = x.shape[1]
    attn_w = d // 2
    h = rmsnorm(x, wts["norm1_g"], min(tm, 256))
    cos, s1, s2 = rope_tables(pos, tm)
    tab_spec = pl.BlockSpec((tm, LANES), lambda n, i: (i, 0))
    vec_spec = pl.BlockSpec((1, tn), lambda n, i: (0, 0))
    bd_spec = pl.BlockSpec((MXU_DIM, MXU_DIM), lambda n, i: (0, 0))
    specs = (vec_spec, bd_spec, tab_spec, tab_spec, tab_spec)
    bd = _block_diag_ones()
    q = mm_fullk([h], wts["w_in"], 0, attn_w, tm=tm, tn=tn, out_dtype=BF16, epilogue="qk",
                 extra=(_tile_lanes(wts["q_norm_g"], tn), bd, cos, s1, s2), extra_specs=specs,
                 scale=D_SUB ** -0.5, name="proj_q")
    k = mm_fullk([h], wts["w_in"], attn_w, attn_w, tm=tm, tn=tn, out_dtype=F32, epilogue="qk",
                 extra=(_tile_lanes(wts["k_norm_g"], tn), bd, cos, s1, s2), extra_specs=specs,
                 name="proj_k")
    v = mm_fullk([h], wts["w_in"], 2 * attn_w, attn_w, tm=tm, tn=tn, out_dtype=F32, name="proj_v")
    u = mm_fullk([h], wts["w_in"], 3 * attn_w, d - attn_w, tm=tm, tn=tn, out_dtype=F32, name="proj_u")
    return q, k, v, u


def finish(x, attn, pool, wts, *, tm, tn, tn_up, tm_down, tn_down, tk_down, sample=None,
           pages_per_step=None):
    d = x.shape[1]
    x1 = mm_fullk([attn, pool], wts["w_out"], 0, d, tm=tm, tn=tn, out_dtype=F32, epilogue="resid",
                  extra=(x,), extra_specs=(pl.BlockSpec((tm, tn), lambda n, i: (i, n)),),
                  name="out_proj")
    h2 = rmsnorm(x1, wts["norm2_g"], min(tm, 256))
    if sample is None:
        a = mm_fullk([h2], wts["w_up"], 0, wts["w_up"].shape[1], tm=tm, tn=tn_up, out_dtype=BF16,
                     epilogue="relu2", name="mlp_up")
        y = mm_ktiled_resid(a, wts["w_down"], x1, tm=tm_down, tn=tn_down, tk=tk_down, name="mlp_down")
        return y, None
    n_seq = sample["q"].shape[0]
    n_lo = n_seq // 2
    a, attn_lo = mlp_matmul_with_paged_attention(
        "up", h2, wts["w_up"], None, sample, 0, n_lo, tm=tm, tn=tn_up, tk=d,
        pages_per_step=pages_per_step)
    y, attn_hi = mlp_matmul_with_paged_attention(
        "down", a, wts["w_down"], x1, sample, n_lo, n_seq - n_lo, tm=tm_down, tn=tn_down, tk=tk_down,
        pages_per_step=pages_per_step)
    return y, jnp.concatenate([attn_lo, attn_hi], axis=0)


def kernel(x_prompt, x_sample, cache_k, cache_v, state_pool, page_table, norm1_g, w_in, q_norm_g,
           k_norm_g, lambda_q1, lambda_k1, lambda_q2, lambda_k2, subln_g, pool_w, pool_scale, w_out,
           norm2_g, w_up, w_down):
    batch, seq, d_model = x_prompt.shape
    dec_batch, dec_seq, _ = x_sample.shape
    assert batch == 1 and dec_seq == 1
    depth = w_in.shape[0]
    n_phys = cache_k.shape[1]
    n_heads = cache_k.shape[3]
    attn_w = n_heads * D_V
    pool_width = d_model - attn_w
    past_len = page_table.shape[1] * PAGE_SIZE

    xp = x_prompt.reshape(seq, d_model)
    xs = x_sample.reshape(dec_batch, d_model)
    pos_p = jnp.arange(seq, dtype=F32).reshape(seq, 1)
    pos_s = jnp.full((dec_batch, 1), past_len, F32)

    outs = [[] for _ in range(6)]
    for l in range(depth):
        lam_init = 0.8 - 0.6 * math.exp(-0.3 * l)
        wts = {
            "norm1_g": norm1_g[l], "q_norm_g": q_norm_g[l], "k_norm_g": k_norm_g[l],
            "norm2_g": norm2_g[l],
            "w_in": w_in[l].astype(BF16),
        }
        pool_w_l = pool_w[l].astype(BF16)
        lam_vecs = [v[l].reshape(1, D_SUB) for v in (lambda_q1, lambda_k1, lambda_q2, lambda_k2)]

        q, k, v, u = project(xp, pos_p, wts, tm=512, tn=1024)
        qs, kn, vn, us = project(xs, pos_s, wts, tm=dec_batch, tn=512)
        to_heads = lambda a: a.reshape(dec_batch, n_heads, D_V)
        sample = {
            "n_heads": n_heads, "page_table": page_table, "page_base": l * n_phys,
            "q": to_heads(qs), "k_new": to_heads(kn), "v_new": to_heads(vn),
            "cache_k": cache_k.reshape(-1, PAGE_SIZE * n_heads, D_V),
            "cache_v": cache_v.reshape(-1, PAGE_SIZE * n_heads, D_V),
            "lam_vecs": lam_vecs, "subln_g": subln_g[l], "lam_init": lam_init,
        }

        attn, (wts["w_out"], wts["w_up"], wts["w_down"]) = flash_diff_attention(
            q, k, v, lam_vecs, subln_g[l], lam_init, (w_out[l], w_up[l], w_down[l]), tq=512, tk=512)
        pool = pool_mix_prompt(u, pool_w_l, pool_scale[l], tm=256)
        xp, attn_s = finish(xp, attn, pool, wts, tm=512, tn=1024, tn_up=1024, tm_down=1024,
                            tn_down=1024, tk_down=2048, sample=sample, pages_per_step=4)
        outs[0].append(k.reshape(batch, seq, n_heads, D_V))
        outs[1].append(v.reshape(batch, seq, n_heads, D_V))
        outs[2].append(u[seq - POOL_BUF:].reshape(batch, POOL_BUF, pool_width))

        state_l = state_pool[l]
        pool_s = pool_mix_sample(jnp.swapaxes(state_l, 0, 1), us, pool_w_l, pool_scale[l], past_len)
        xs, _ = finish(xs, attn_s, pool_s, wts, tm=dec_batch, tn=1024, tn_up=1024, tm_down=dec_batch,
                       tn_down=2048, tk_down=2048)
        outs[3].append(kn.reshape(dec_batch, dec_seq, n_heads, D_V))
        outs[4].append(vn.reshape(dec_batch, dec_seq, n_heads, D_V))
        outs[5].append(jnp.concatenate([state_l[:, 1:], us[:, None, :]], axis=1))

    k_prompt, v_prompt, pool_prompt, k_sample, v_sample, pool_sample = (jnp.stack(o) for o in outs)
    return (xp.reshape(batch, seq, d_model), xs.reshape(dec_batch, dec_seq, d_model),
            k_prompt, v_prompt, pool_prompt, k_sample, v_sample, pool_sample)
```

```python
import functools
import math

import jax
import jax.numpy as jnp
from jax import lax
from jax.experimental import pallas as pl
from jax.experimental.pallas import tpu as pltpu

PAGE_SIZE = 128
D_V = 128
D_SUB = D_V // 2
ROT_DIM = D_SUB // 4
ROPE_THETA = 500000.0
POOL_WINDOWS = (2, 4, 8, 16)
POOL_BUF = max(POOL_WINDOWS) - 1
EPS = 1e-6
NEG_INF = -1e30

LANES = 128
MXU_DIM = 256
ONES_ROWS = 16
VMEM_LIMIT_BYTES = 56 * 1024 * 1024

F32 = jnp.float32
BF16 = jnp.bfloat16


def _params(*sem):
    return pltpu.CompilerParams(dimension_semantics=sem, vmem_limit_bytes=VMEM_LIMIT_BYTES)


def _rmsnorm_kernel(x_ref, g_ref, o_ref):
    x = x_ref[...]
    ms = jnp.mean(x * x, axis=-1, keepdims=True)
    o_ref[...] = (x * lax.rsqrt(ms + EPS) * g_ref[...]).astype(o_ref.dtype)


def rmsnorm(x, g, tm):
    m, d = x.shape
    return pl.pallas_call(
        _rmsnorm_kernel,
        grid=(m // tm,),
        in_specs=[pl.BlockSpec((tm, d), lambda i: (i, 0)),
                  pl.BlockSpec((1, d), lambda i: (0, 0))],
        out_specs=pl.BlockSpec((tm, d), lambda i: (i, 0)),
        out_shape=jax.ShapeDtypeStruct((m, d), BF16),
        compiler_params=_params("arbitrary"),
        name="rmsnorm",
    )(x, g.reshape(1, d))


def _rope_table_kernel(pos_ref, invf_ref, c_ref, s1_ref, s2_ref):
    ang = pos_ref[...] * invf_ref[...]
    c = jnp.cos(ang)
    s = jnp.sin(ang)
    r = lax.broadcasted_iota(jnp.int32, ang.shape, 1) & (D_SUB - 1)
    c_ref[...] = c
    s1_ref[...] = jnp.where(r < ROT_DIM // 2, -s, 0.0)
    s2_ref[...] = jnp.where((r >= ROT_DIM // 2) & (r < ROT_DIM), s, 0.0)


def rope_tables(pos, tm):
    m = pos.shape[0]
    inv_freq = ROPE_THETA ** (-jnp.arange(0, ROT_DIM, 2, dtype=F32) / ROT_DIM)
    sub = jnp.concatenate([inv_freq, inv_freq, jnp.zeros((D_SUB - ROT_DIM,), F32)])
    invf_lane = jnp.tile(sub, LANES // D_SUB).reshape(1, LANES)
    spec = pl.BlockSpec((tm, LANES), lambda i: (i, 0))
    shp = jax.ShapeDtypeStruct((m, LANES), F32)
    return pl.pallas_call(
        _rope_table_kernel,
        grid=(m // tm,),
        in_specs=[pl.BlockSpec((tm, 1), lambda i: (i, 0)),
                  pl.BlockSpec((1, LANES), lambda i: (0, 0))],
        out_specs=[spec, spec, spec],
        out_shape=[shp, shp, shp],
        compiler_params=_params("arbitrary"),
        name="rope_tables",
    )(pos, invf_lane)


def _qk_norm_rope(z, g_ref, bd_ref, c_ref, s1_ref, s2_ref, scale):
    tn = z.shape[1]
    zz = z * z
    hi = zz.astype(BF16)
    lo = (zz - hi.astype(F32)).astype(BF16)
    bd = bd_ref[...]
    parts = []
    for c in range(tn // MXU_DIM):
        sl = slice(c * MXU_DIM, (c + 1) * MXU_DIM)
        parts.append(jnp.dot(hi[:, sl], bd, preferred_element_type=F32)
                     + jnp.dot(lo[:, sl], bd, preferred_element_type=F32))
    ss = jnp.concatenate(parts, axis=1) if len(parts) > 1 else parts[0]
    y = z * lax.rsqrt(ss * (1.0 / D_SUB) + EPS) * g_ref[...]
    cos, s1, s2 = c_ref[...], s1_ref[...], s2_ref[...]
    outs = []
    for c in range(tn // LANES):
        ys = y[:, c * LANES:(c + 1) * LANES]
        up = pltpu.roll(ys, LANES - ROT_DIM // 2, axis=1)
        dn = pltpu.roll(ys, ROT_DIM // 2, axis=1)
        outs.append(ys * cos + up * s1 + dn * s2)
    out = jnp.concatenate(outs, axis=1) if len(outs) > 1 else outs[0]
    if scale != 1.0:
        out = out * scale
    return out


def _mm_fullk_kernel(*refs, n_lhs, epilogue, scale, layouts):
    x_refs = refs[:n_lhs]
    w_ref = refs[n_lhs]
    extra = refs[n_lhs + 1:-len(layouts)]
    out_refs = refs[-len(layouts):]
    kk = w_ref.shape[0] // n_lhs
    acc = None
    for i, x_ref in enumerate(x_refs):
        part = jnp.dot(x_ref[...], w_ref[i * kk:(i + 1) * kk, :], preferred_element_type=F32)
        acc = part if acc is None else acc + part
    if epilogue == "relu2":
        a = jnp.maximum(acc, 0.0)
        acc = a * a
    elif epilogue == "resid":
        acc = extra[0][...] + acc
    elif epilogue == "qk":
        acc = _qk_norm_rope(acc, *extra, scale)
    for layout, o_ref in zip(layouts, out_refs):
        if layout == "rows":
            o_ref[...] = acc.astype(o_ref.dtype)
        else:
            for hh in range(o_ref.shape[0]):
                o_ref[hh] = acc[:, hh * D_V:(hh + 1) * D_V].astype(o_ref.dtype)


def mm_fullk(xs, w, col_off, n_cols, *, tm, tn, out_dtype, epilogue=None, extra=(), extra_specs=(),
             scale=1.0, layouts=("rows",), name="mm"):
    m = xs[0].shape[0]
    k_total = w.shape[0]
    off = col_off // tn
    x_specs = [pl.BlockSpec((tm, x.shape[1]), lambda n, i: (i, 0)) for x in xs]
    w_spec = pl.BlockSpec((k_total, tn), lambda n, i: (0, n + off))
    out_specs, out_shape = [], []
    for layout in layouts:
        if layout == "rows":
            out_specs.append(pl.BlockSpec((tm, tn), lambda n, i: (i, n)))
            out_shape.append(jax.ShapeDtypeStruct((m, n_cols), out_dtype))
        else:
            out_specs.append(pl.BlockSpec((tn // D_V, tm, D_V), lambda n, i: (n, i, 0)))
            out_shape.append(jax.ShapeDtypeStruct((n_cols // D_V, m, D_V), BF16))
    outs = pl.pallas_call(
        functools.partial(_mm_fullk_kernel, n_lhs=len(xs), epilogue=epilogue, scale=scale,
                          layouts=layouts),
        grid=(n_cols // tn, m // tm),
        in_specs=x_specs + [w_spec] + list(extra_specs),
        out_specs=out_specs,
        out_shape=out_shape,
        compiler_params=_params("arbitrary", "arbitrary"),
        name=name,
    )(*xs, w, *extra)
    return outs[0] if len(layouts) == 1 else outs


def _mm_ktiled_kernel(x_ref, w_ref, r_ref, o_ref):
    part = jnp.dot(x_ref[...], w_ref[...], preferred_element_type=F32)

    @pl.when(pl.program_id(2) == 0)
    def _():
        o_ref[...] = r_ref[...] + part

    @pl.when(pl.program_id(2) > 0)
    def _():
        o_ref[...] += part


def mm_ktiled_resid(x, w, resid, *, tm, tn, tk, name="mm_ktiled"):
    m, kdim = x.shape
    n = w.shape[1]
    nk = kdim // tk
    return pl.pallas_call(
        _mm_ktiled_kernel,
        grid=(m // tm, n // tn, nk),
        in_specs=[pl.BlockSpec((tm, tk), lambda i, j, k: (i, k)),
                  pl.BlockSpec((tk, tn), lambda i, j, k: (k, j)),
                  pl.BlockSpec((tm, tn), lambda i, j, k: (i, j))],
        out_specs=pl.BlockSpec((tm, tn), lambda i, j, k: (i, j)),
        out_shape=jax.ShapeDtypeStruct((m, n), F32),
        compiler_params=_params("arbitrary", "arbitrary", "arbitrary"),
        name=name,
    )(x, w, resid)


def _diff_lambda(lq1_ref, lk1_ref, lq2_ref, lk2_ref, lam_init):
    a = jnp.sum(lq1_ref[...] * lk1_ref[...], axis=-1, keepdims=True)
    b = jnp.sum(lq2_ref[...] * lk2_ref[...], axis=-1, keepdims=True)
    return jnp.exp(a) - jnp.exp(b) + lam_init


def _flash_kernel(q_ref, k_ref, v_ref, lq1_ref, lk1_ref, lq2_ref, lk2_ref, sg_ref, *rest,
                  n_cast, tq, tk, lam_init):
    cast_in, o_ref, cast_out = rest[:n_cast], rest[n_cast], rest[n_cast + 1:2 * n_cast + 1]
    vt_ref, m_ref, acc_ref, st_ref = rest[2 * n_cast + 1:]
    qi = pl.program_id(1)
    n_chunks = vt_ref.shape[0]

    for w_ref, wb_ref in zip(cast_in, cast_out):
        wb_ref[...] = w_ref[...].astype(BF16)

    @pl.when(qi == 0)
    def _():
        vt_ref[:, D_V:, :] = jnp.ones((n_chunks, ONES_ROWS, tk), BF16)
        for c in range(n_chunks):
            vt_ref[c, :D_V, :] = v_ref[c * tk:(c + 1) * tk, :].astype(F32).T.astype(BF16)

    q = q_ref[...].astype(F32)
    lane = lax.broadcasted_iota(jnp.int32, q.shape, 1)
    q2 = jnp.concatenate([jnp.where(lane < D_SUB, q, 0.0),
                          jnp.where(lane >= D_SUB, q, 0.0)], axis=0).astype(BF16)

    m_ref[...] = jnp.full_like(m_ref, NEG_INF)
    acc_ref[...] = jnp.zeros_like(acc_ref)

    def scores(j, slot):
        start = pl.multiple_of(j * tk, tk)
        kc = k_ref[pl.ds(start, tk), :]
        st_ref[slot] = lax.dot_general(kc, q2, (((1,), (1,)), ((), ())),
                                       preferred_element_type=F32)

    def consume(j, slot, masked):
        st = st_ref[slot]
        if masked:
            key = lax.broadcasted_iota(jnp.int32, st.shape, 0) + j * tk
            qpos = (lax.broadcasted_iota(jnp.int32, st.shape, 1) & (tq - 1)) + qi * tq
            st = jnp.where(key <= qpos, st, NEG_INF)
        m_prev = m_ref[...]
        m_new = jnp.maximum(m_prev, jnp.max(st, axis=0, keepdims=True))
        alpha = jnp.exp(m_prev - m_new)
        pt = jnp.exp(st - m_new)
        pv = jnp.dot(vt_ref[j], pt.astype(BF16), preferred_element_type=F32)
        acc_ref[...] = alpha * acc_ref[...] + pv
        m_ref[...] = m_new

    n_full = (qi * tq) // tk
    scores(0, 0)

    def body(jj, carry):
        scores(2 * jj + 1, 1)
        consume(2 * jj, 0, False)
        scores(2 * jj + 2, 0)
        consume(2 * jj + 1, 1, False)
        return carry

    lax.fori_loop(0, n_full // 2, body, 0)

    @pl.when(n_full % 2 == 1)
    def _():
        scores(n_full, 1)
        consume(n_full - 1, 0, False)
        consume(n_full, 1, True)

    @pl.when(n_full % 2 == 0)
    def _():
        consume(n_full, 0, True)

    acc = acc_ref[:D_V, :]
    l = acc_ref[D_V:D_V + 1, :]
    lam = _diff_lambda(lq1_ref, lk1_ref, lq2_ref, lk2_ref, lam_init)
    o = acc[:, :tq] / l[:, :tq] - lam * (acc[:, tq:] / l[:, tq:])
    ms = jnp.mean(o * o, axis=0, keepdims=True)
    y = o * lax.rsqrt(ms + EPS) * sg_ref[...] * (1.0 - lam_init)
    o_ref[...] = y.T.astype(o_ref.dtype)


def flash_diff_attention(q, k, v, lam_vecs, subln_g, lam_init, cast_weights, *, tq, tk):
    n_heads, s, _ = q.shape
    width = n_heads * D_V
    n_q = s // tq
    assert tk % tq == 0 and s % tk == 0
    vec = pl.BlockSpec((1, D_SUB), lambda h, i: (0, 0))
    cast_specs = []
    for w in cast_weights:
        assert w.shape[0] % n_heads == 0 and w.shape[1] % n_q == 0
        cast_specs.append(pl.BlockSpec((w.shape[0] // n_heads, w.shape[1] // n_q), lambda h, i: (h, i)))
    outs = pl.pallas_call(
        functools.partial(_flash_kernel, n_cast=len(cast_weights), tq=tq, tk=tk, lam_init=lam_init),
        grid=(n_heads, n_q),
        in_specs=[pl.BlockSpec((None, tq, D_V), lambda h, i: (h, i, 0)),
                  pl.BlockSpec((None, s, D_V), lambda h, i: (h, 0, 0)),
                  pl.BlockSpec((None, s, D_V), lambda h, i: (h, 0, 0)),
                  vec, vec, vec, vec,
                  pl.BlockSpec((D_V, 1), lambda h, i: (0, 0))] + cast_specs,
        out_specs=[pl.BlockSpec((tq, D_V), lambda h, i: (i, h))] + cast_specs,
        out_shape=[jax.ShapeDtypeStruct((s, width), BF16)]
                  + [jax.ShapeDtypeStruct(w.shape, BF16) for w in cast_weights],
        scratch_shapes=[pltpu.VMEM((s // tk, D_V + ONES_ROWS, tk), BF16),
                        pltpu.VMEM((1, 2 * tq), F32),
                        pltpu.VMEM((D_V + ONES_ROWS, 2 * tq), F32),
                        pltpu.VMEM((2, tk, 2 * tq), F32)],
        compiler_params=_params("arbitrary", "arbitrary"),
        name="flash_diff_attention",
    )(q, k, v, *lam_vecs, subln_g.reshape(D_V, 1), *cast_weights)
    return outs[0], outs[1:]


def _paged_update(p_idx, n_steps, q_ref, kn_ref, vn_ref, k_refs, v_refs, lam_refs, sg_ref, o_ref,
                  scratch, host_work, *, n_heads, lam_init):
    lq1_ref, lk1_ref, lq2_ref, lk2_ref = lam_refs
    q2_ref, bias_ref, m_ref, l_ref, acc_ref = scratch

    @pl.when(p_idx == 0)
    def _():
        q = q_ref[...].astype(F32)
        lane = lax.broadcasted_iota(jnp.int32, q.shape, 1)
        q2_ref[...] = jnp.concatenate([jnp.where(lane < D_SUB, q, 0.0),
                                       jnp.where(lane >= D_SUB, q, 0.0)], axis=0).astype(BF16)
        row = lax.broadcasted_iota(jnp.int32, bias_ref.shape, 0)
        col = lax.broadcasted_iota(jnp.int32, bias_ref.shape, 1)
        same_head = (col & (n_heads - 1)) == (row & (n_heads - 1))
        bias_ref[...] = jnp.where(same_head, 0.0, NEG_INF)
        m_ref[...] = jnp.full_like(m_ref, NEG_INF)
        l_ref[...] = jnp.zeros_like(l_ref)
        acc_ref[...] = jnp.zeros_like(acc_ref)

    q2 = q2_ref[...]
    bias = bias_ref[...]
    s_parts = [lax.dot_general(q2, k_ref[...].astype(BF16), (((1,), (1,)), ((), ())),
                               preferred_element_type=F32) + bias for k_ref in k_refs]
    host_work()
    m_prev = m_ref[...]
    m_new = m_prev
    for s in s_parts:
        m_new = jnp.maximum(m_new, jnp.max(s, axis=1, keepdims=True))
    alpha = jnp.exp(m_prev - m_new)
    l_new = alpha * l_ref[...]
    acc = alpha * acc_ref[...]
    for s, v_ref in zip(s_parts, v_refs):
        p = jnp.exp(s - m_new)
        l_new = l_new + jnp.sum(p, axis=1, keepdims=True)
        acc = acc + jnp.dot(p.astype(BF16), v_ref[...].astype(BF16), preferred_element_type=F32)
    l_ref[...] = l_new
    acc_ref[...] = acc
    m_ref[...] = m_new

    @pl.when(p_idx == n_steps - 1)
    def _():
        kn2 = jnp.concatenate([kn_ref[...], kn_ref[...]], axis=0)
        vn2 = jnp.concatenate([vn_ref[...], vn_ref[...]], axis=0)
        s_new = jnp.sum(q2_ref[...].astype(F32) * kn2, axis=1, keepdims=True)
        m_old = m_ref[...]
        m_fin = jnp.maximum(m_old, s_new)
        a = jnp.exp(m_old - m_fin)
        p_new = jnp.exp(s_new - m_fin)
        l = a * l_ref[...] + p_new
        acc_f = a * acc_ref[...] + p_new * vn2
        lam = _diff_lambda(lq1_ref, lk1_ref, lq2_ref, lk2_ref, lam_init)
        o = acc_f[:n_heads] / l[:n_heads] - lam * (acc_f[n_heads:] / l[n_heads:])
        ms = jnp.mean(o * o, axis=1, keepdims=True)
        y = o * lax.rsqrt(ms + EPS) * sg_ref[...] * (1.0 - lam_init)
        o_ref[...] = y.astype(o_ref.dtype)


def _mlp_paged_kernel(pt_ref, *refs, host, grid_dims, pp, steps_per_seq, n_heads, lam_init):
    del pt_ref
    n_host = 2 if host == "up" else 3
    host_in, rest = refs[:n_host], refs[n_host:]
    q_ref, kn_ref, vn_ref = rest[:3]
    k_refs, v_refs = rest[3:3 + pp], rest[3 + pp:3 + 2 * pp]
    lam_refs = rest[3 + 2 * pp:7 + 2 * pp]
    sg_ref, o_ref, attn_ref = rest[7 + 2 * pp:10 + 2 * pp]
    scratch = rest[10 + 2 * pp:]

    parts = []

    def host_work():
        part = jnp.dot(host_in[0][...], host_in[1][...], preferred_element_type=F32)
        if host == "up":
            a = jnp.maximum(part, 0.0)
            o_ref[...] = (a * a).astype(o_ref.dtype)
        else:
            parts.append(part)

    step = pl.program_id(0)
    for axis in range(1, len(grid_dims)):
        step = step * grid_dims[axis] + pl.program_id(axis)
    _paged_update(lax.rem(step, steps_per_seq), steps_per_seq, q_ref, kn_ref, vn_ref, k_refs, v_refs,
                  lam_refs, sg_ref, attn_ref, scratch, host_work, n_heads=n_heads, lam_init=lam_init)

    if host == "down":
        @pl.when(pl.program_id(2) == 0)
        def _():
            o_ref[...] = host_in[2][...] + parts[0]

        @pl.when(pl.program_id(2) > 0)
        def _():
            o_ref[...] += parts[0]


def mlp_matmul_with_paged_attention(host, x, w, resid, sample, seq_lo, n_seq, *, tm, tn, tk,
                                    pages_per_step):
    m, kdim = x.shape
    n = w.shape[1]
    pp = pages_per_step
    n_heads = sample["n_heads"]
    page_table = sample["page_table"]
    steps_per_seq = page_table.shape[1] // pp
    rows = 2 * n_heads
    page_rows = PAGE_SIZE * n_heads
    if host == "up":
        grid = (n // tn, m // tm)
        assert tk == kdim
        host_specs = [pl.BlockSpec((tm, kdim), lambda j, i, pt: (i, 0)),
                      pl.BlockSpec((kdim, tn), lambda j, i, pt: (0, j))]
        host_out = pl.BlockSpec((tm, tn), lambda j, i, pt: (i, j))
        host_args, out_dtype = (x, w), BF16
    else:
        grid = (m // tm, n // tn, kdim // tk)
        host_specs = [pl.BlockSpec((tm, tk), lambda i, j, k, pt: (i, k)),
                      pl.BlockSpec((tk, tn), lambda i, j, k, pt: (k, j)),
                      pl.BlockSpec((tm, tn), lambda i, j, k, pt: (i, j))]
        host_out = pl.BlockSpec((tm, tn), lambda i, j, k, pt: (i, j))
        host_args, out_dtype = (x, w, resid), F32
    n_steps = math.prod(grid)
    assert n_steps == n_seq * steps_per_seq, (grid, n_seq, steps_per_seq)

    def linear(ids):
        step = ids[0]
        for axis in range(1, len(grid)):
            step = step * grid[axis] + ids[axis]
        return step

    def seq_map(*args):
        return (seq_lo + linear(args[:-1]) // steps_per_seq, 0, 0)

    def page_map(c):
        def index(*args):
            step, pt = linear(args[:-1]), args[-1]
            seq = seq_lo + step // steps_per_seq
            return (sample["page_base"] + pt[seq, (step % steps_per_seq) * pp + c], 0, 0)
        return index

    def const_map(*args):
        return (0, 0)

    seq_spec = pl.BlockSpec((None, n_heads, D_V), seq_map)
    page_specs = [pl.BlockSpec((None, page_rows, D_V), page_map(c)) for c in range(pp)]
    vec = pl.BlockSpec((1, D_SUB), const_map)
    out, attn = pl.pallas_call(
        functools.partial(_mlp_paged_kernel, host=host, grid_dims=grid, pp=pp,
                          steps_per_seq=steps_per_seq, n_heads=n_heads, lam_init=sample["lam_init"]),
        grid_spec=pltpu.PrefetchScalarGridSpec(
            num_scalar_prefetch=1,
            grid=grid,
            in_specs=host_specs + [seq_spec, seq_spec, seq_spec] + page_specs + page_specs
                     + [vec, vec, vec, vec, pl.BlockSpec((1, D_V), const_map)],
            out_specs=[host_out,
                       pl.BlockSpec((None, n_heads, D_V),
                                    lambda *args: (linear(args[:-1]) // steps_per_seq, 0, 0))],
            scratch_shapes=[pltpu.VMEM((rows, D_V), BF16),
                            pltpu.VMEM((rows, page_rows), F32),
                            pltpu.VMEM((rows, 1), F32), pltpu.VMEM((rows, 1), F32),
                            pltpu.VMEM((rows, D_V), F32)]),
        out_shape=[jax.ShapeDtypeStruct((m, n), out_dtype),
                   jax.ShapeDtypeStruct((n_seq, n_heads, D_V), BF16)],
        compiler_params=_params(*(["arbitrary"] * len(grid))),
        name="mlp_" + host + "_paged_attention",
    )(page_table, *host_args, sample["q"], sample["k_new"], sample["v_new"],
      *([sample["cache_k"]] * pp), *([sample["cache_v"]] * pp), *sample["lam_vecs"],
      sample["subln_g"].reshape(1, D_V))
    return out, attn.reshape(n_seq, n_heads * D_V)


HALO = POOL_BUF + 1


def _pool_prompt_kernel(u_ref, halo_ref, w_ref, sc_ref, o_ref, ext_ref, *, tm, group):
    i = pl.program_id(0)
    halo = halo_ref[...]
    ext_ref[0:HALO, :] = jnp.where(i > 0, halo, jnp.zeros_like(halo))
    ext_ref[HALO:, :] = u_ref[...]
    pos = i * tm + lax.broadcasted_iota(jnp.int32, (tm, 1), 0)
    for g, win in enumerate(POOL_WINDOWS):
        cols = slice(g * group, (g + 1) * group)
        cur = u_ref[:, cols]
        tot = cur
        for k in range(1, win):
            tot = tot + ext_ref[HALO - k:HALO - k + tm, cols]
        cnt = jnp.minimum(win, pos + 1).astype(F32)
        d = tot / cnt - cur
        y = jnp.dot(d.astype(BF16), w_ref[g], preferred_element_type=F32) * sc_ref[:, cols]
        o_ref[:, cols] = y.astype(o_ref.dtype)


def pool_mix_prompt(u, pool_w, pool_scale, *, tm):
    s, width = u.shape
    group = width // len(POOL_WINDOWS)
    ratio = tm // HALO
    return pl.pallas_call(
        functools.partial(_pool_prompt_kernel, tm=tm, group=group),
        grid=(s // tm,),
        in_specs=[pl.BlockSpec((tm, width), lambda i: (i, 0)),
                  pl.BlockSpec((HALO, width), lambda i: (jnp.maximum(i * ratio - 1, 0), 0)),
                  pl.BlockSpec(pool_w.shape, lambda i: (0, 0, 0)),
                  pl.BlockSpec((1, width), lambda i: (0, 0))],
        out_specs=pl.BlockSpec((tm, width), lambda i: (i, 0)),
        out_shape=jax.ShapeDtypeStruct((s, width), BF16),
        scratch_shapes=[pltpu.VMEM((tm + HALO, width), F32)],
        compiler_params=_params("arbitrary"),
        name="pool_mix_prompt",
    )(u, u, pool_w, pool_scale.reshape(1, width))


def _pool_sample_kernel(st_ref, us_ref, w_ref, sc_ref, o_ref, *, group, first_pos):
    cur_all = us_ref[...]
    for g, win in enumerate(POOL_WINDOWS):
        cols = slice(g * group, (g + 1) * group)
        cur = cur_all[:, cols]
        tot = cur
        for k in range(1, win):
            tot = tot + st_ref[POOL_BUF - k, :, cols]
        cnt = float(min(win, first_pos + 1))
        d = tot / cnt - cur
        y = jnp.dot(d.astype(BF16), w_ref[g], preferred_element_type=F32) * sc_ref[:, cols]
        o_ref[:, cols] = y.astype(o_ref.dtype)


def pool_mix_sample(state_t, us, pool_w, pool_scale, first_pos):
    b, width = us.shape
    group = width // len(POOL_WINDOWS)
    return pl.pallas_call(
        functools.partial(_pool_sample_kernel, group=group, first_pos=first_pos),
        grid=(1,),
        in_specs=[pl.BlockSpec(state_t.shape, lambda i: (0, 0, 0)),
                  pl.BlockSpec((b, width), lambda i: (0, 0)),
                  pl.BlockSpec(pool_w.shape, lambda i: (0, 0, 0)),
                  pl.BlockSpec((1, width), lambda i: (0, 0))],
        out_specs=pl.BlockSpec((b, width), lambda i: (0, 0)),
        out_shape=jax.ShapeDtypeStruct((b, width), BF16),
        compiler_params=_params("arbitrary"),
        name="pool_mix_sample",
    )(state_t, us, pool_w, pool_scale.reshape(1, width))


def _block_diag_ones():
    r = jnp.arange(MXU_DIM) // D_SUB
    return (r[:, None] == r[None, :]).astype(BF16)


def _tile_lanes(vec, n):
    return jnp.tile(vec, n // vec.shape[0]).reshape(1, n)


def project(x, pos, wts, *, tm, tn, head_major=False):
    q_layouts = ("heads",) if head_major else ("rows",)
    kv_layouts = ("rows", "heads") if head_major else ("rows",)
    d = x.shape[1]
    attn_w = d // 2
    h = rmsnorm(x, wts["norm1_g"], min(tm, 256))
    cos, s1, s2 = rope_tables(pos, tm)
    tab_spec = pl.BlockSpec((tm, LANES), lambda n, i: (i, 0))
    vec_spec = pl.BlockSpec((1, tn), lambda n, i: (0, 0))
    bd_spec = pl.BlockSpec((MXU_DIM, MXU_DIM), lambda n, i: (0, 0))
    specs = (vec_spec, bd_spec, tab_spec, tab_spec, tab_spec)
    bd = _block_diag_ones()
    q = mm_fullk([h], wts["w_in"], 0, attn_w, tm=tm, tn=tn, out_dtype=BF16, epilogue="qk",
                 extra=(_tile_lanes(wts["q_norm_g"], tn), bd, cos, s1, s2), extra_specs=specs,
                 scale=D_SUB ** -0.5, layouts=q_layouts, name="proj_q")
    k = mm_fullk([h], wts["w_in"], attn_w, attn_w, tm=tm, tn=tn, out_dtype=F32, epilogue="qk",
                 extra=(_tile_lanes(wts["k_norm_g"], tn), bd, cos, s1, s2), extra_specs=specs,
                 layouts=kv_layouts, name="proj_k")
    v = mm_fullk([h], wts["w_in"], 2 * attn_w, attn_w, tm=tm, tn=tn, out_dtype=F32,
                 layouts=kv_layouts, name="proj_v")
    u = mm_fullk([h], wts["w_in"], 3 * attn_w, d - attn_w, tm=tm, tn=tn, out_dtype=F32, name="proj_u")
    return q, k, v, u


def finish(x, attn, pool, wts, *, tm, tn, tn_up, tm_down, tn_down, tk_down, sample=None,
           pages_per_step=None):
    d = x.shape[1]
    x1 = mm_fullk([attn, pool], wts["w_out"], 0, d, tm=tm, tn=tn, out_dtype=F32, epilogue="resid",
                  extra=(x,), extra_specs=(pl.BlockSpec((tm, tn), lambda n, i: (i, n)),),
                  name="out_proj")
    h2 = rmsnorm(x1, wts["norm2_g"], min(tm, 256))
    if sample is None:
        a = mm_fullk([h2], wts["w_up"], 0, wts["w_up"].shape[1], tm=tm, tn=tn_up, out_dtype=BF16,
                     epilogue="relu2", name="mlp_up")
        y = mm_ktiled_resid(a, wts["w_down"], x1, tm=tm_down, tn=tn_down, tk=tk_down, name="mlp_down")
        return y, None
    n_seq = sample["q"].shape[0]
    n_lo = n_seq // 2
    a, attn_lo = mlp_matmul_with_paged_attention(
        "up", h2, wts["w_up"], None, sample, 0, n_lo, tm=tm, tn=tn_up, tk=d,
        pages_per_step=pages_per_step)
    y, attn_hi = mlp_matmul_with_paged_attention(
        "down", a, wts["w_down"], x1, sample, n_lo, n_seq - n_lo, tm=tm_down, tn=tn_down, tk=tk_down,
        pages_per_step=pages_per_step)
    return y, jnp.concatenate([attn_lo, attn_hi], axis=0)


def kernel(x_prompt, x_sample, cache_k, cache_v, state_pool, page_table, norm1_g, w_in, q_norm_g,
           k_norm_g, lambda_q1, lambda_k1, lambda_q2, lambda_k2, subln_g, pool_w, pool_scale, w_out,
           norm2_g, w_up, w_down):
    batch, seq, d_model = x_prompt.shape
    dec_batch, dec_seq, _ = x_sample.shape
    assert batch == 1 and dec_seq == 1
    depth = w_in.shape[0]
    n_phys = cache_k.shape[1]
    n_heads = cache_k.shape[3]
    attn_w = n_heads * D_V
    pool_width = d_model - attn_w
    past_len = page_table.shape[1] * PAGE_SIZE

    xp = x_prompt.reshape(seq, d_model)
    xs = x_sample.reshape(dec_batch, d_model)
    pos_p = jnp.arange(seq, dtype=F32).reshape(seq, 1)
    pos_s = jnp.full((dec_batch, 1), past_len, F32)

    outs = [[] for _ in range(6)]
    for l in range(depth):
        lam_init = 0.8 - 0.6 * math.exp(-0.3 * l)
        wts = {
            "norm1_g": norm1_g[l], "q_norm_g": q_norm_g[l], "k_norm_g": k_norm_g[l],
            "norm2_g": norm2_g[l],
            "w_in": w_in[l].astype(BF16),
        }
        pool_w_l = pool_w[l].astype(BF16)
        lam_vecs = [v[l].reshape(1, D_SUB) for v in (lambda_q1, lambda_k1, lambda_q2, lambda_k2)]

        q_hm, (k, k_hm), (v, v_hm), u = project(xp, pos_p, wts, tm=512, tn=1024, head_major=True)
        qs, kn, vn, us = project(xs, pos_s, wts, tm=dec_batch, tn=512)
        to_heads = lambda a: a.reshape(dec_batch, n_heads, D_V)
        sample = {
            "n_heads": n_heads, "page_table": page_table, "page_base": l * n_phys,
            "q": to_heads(qs), "k_new": to_heads(kn), "v_new": to_heads(vn),
            "cache_k": cache_k.reshape(-1, PAGE_SIZE * n_heads, D_V),
            "cache_v": cache_v.reshape(-1, PAGE_SIZE * n_heads, D_V),
            "lam_vecs": lam_vecs, "subln_g": subln_g[l], "lam_init": lam_init,
        }

        attn, (wts["w_out"], wts["w_up"], wts["w_down"]) = flash_diff_attention(
            q_hm, k_hm, v_hm, lam_vecs, subln_g[l], lam_init, (w_out[l], w_up[l], w_down[l]),
            tq=512, tk=512)
        pool = pool_mix_prompt(u, pool_w_l, pool_scale[l], tm=256)
        xp, attn_s = finish(xp, attn, pool, wts, tm=512, tn=1024, tn_up=1024, tm_down=1024,
                            tn_down=1024, tk_down=2048, sample=sample, pages_per_step=4)
        outs[0].append(k.reshape(batch, seq, n_heads, D_V))
        outs[1].append(v.reshape(batch, seq, n_heads, D_V))
        outs[2].append(u[seq - POOL_BUF:].reshape(batch, POOL_BUF, pool_width))

        state_l = state_pool[l]
        pool_s = pool_mix_sample(jnp.swapaxes(state_l, 0, 1), us, pool_w_l, pool_scale[l], past_len)
        xs, _ = finish(xs, attn_s, pool_s, wts, tm=dec_batch, tn=1024, tn_up=1024, tm_down=dec_batch,
                       tn_down=2048, tk_down=2048)
        outs[3].append(kn.reshape(dec_batch, dec_seq, n_heads, D_V))
        outs[4].append(vn.reshape(dec_batch, dec_seq, n_heads, D_V))
        outs[5].append(jnp.concatenate([state_l[:, 1:], us[:, None, :]], axis=1))

    k_prompt, v_prompt, pool_prompt, k_sample, v_sample, pool_sample = (jnp.stack(o) for o in outs)
    return (xp.reshape(batch, seq, d_model), xs.reshape(dec_batch, dec_seq, d_model),
            k_prompt, v_prompt, pool_prompt, k_sample, v_sample, pool_sample)
```

```python
import functools
import math

import jax
import jax.numpy as jnp
from jax import lax
from jax.experimental import pallas as pl
from jax.experimental.pallas import tpu as pltpu

PAGE_SIZE = 128
D_V = 128
D_SUB = D_V // 2
ROT_DIM = D_SUB // 4
ROPE_THETA = 500000.0
POOL_WINDOWS = (2, 4, 8, 16)
POOL_BUF = max(POOL_WINDOWS) - 1
EPS = 1e-6
NEG_INF = -1e30

LANES = 128
MXU_DIM = 256
ONES_ROWS = 16
VMEM_LIMIT_BYTES = 56 * 1024 * 1024

F32 = jnp.float32
BF16 = jnp.bfloat16


def _params(*sem):
    return pltpu.CompilerParams(dimension_semantics=sem, vmem_limit_bytes=VMEM_LIMIT_BYTES)


def _rmsnorm_kernel(x_ref, g_ref, o_ref):
    x = x_ref[...]
    ms = jnp.mean(x * x, axis=-1, keepdims=True)
    o_ref[...] = (x * lax.rsqrt(ms + EPS) * g_ref[...]).astype(o_ref.dtype)


def rmsnorm(x, g, tm):
    m, d = x.shape
    return pl.pallas_call(
        _rmsnorm_kernel,
        grid=(m // tm,),
        in_specs=[pl.BlockSpec((tm, d), lambda i: (i, 0)),
                  pl.BlockSpec((1, d), lambda i: (0, 0))],
        out_specs=pl.BlockSpec((tm, d), lambda i: (i, 0)),
        out_shape=jax.ShapeDtypeStruct((m, d), BF16),
        compiler_params=_params("arbitrary"),
        name="rmsnorm",
    )(x, g.reshape(1, d))


def _rope_table_kernel(pos_ref, invf_ref, c_ref, s1_ref, s2_ref):
    ang = pos_ref[...] * invf_ref[...]
    c = jnp.cos(ang)
    s = jnp.sin(ang)
    r = lax.broadcasted_iota(jnp.int32, ang.shape, 1) & (D_SUB - 1)
    c_ref[...] = c
    s1_ref[...] = jnp.where(r < ROT_DIM // 2, -s, 0.0)
    s2_ref[...] = jnp.where((r >= ROT_DIM // 2) & (r < ROT_DIM), s, 0.0)


def rope_tables(pos, tm):
    m = pos.shape[0]
    inv_freq = ROPE_THETA ** (-jnp.arange(0, ROT_DIM, 2, dtype=F32) / ROT_DIM)
    sub = jnp.concatenate([inv_freq, inv_freq, jnp.zeros((D_SUB - ROT_DIM,), F32)])
    invf_lane = jnp.tile(sub, LANES // D_SUB).reshape(1, LANES)
    spec = pl.BlockSpec((tm, LANES), lambda i: (i, 0))
    shp = jax.ShapeDtypeStruct((m, LANES), F32)
    return pl.pallas_call(
        _rope_table_kernel,
        grid=(m // tm,),
        in_specs=[pl.BlockSpec((tm, 1), lambda i: (i, 0)),
                  pl.BlockSpec((1, LANES), lambda i: (0, 0))],
        out_specs=[spec, spec, spec],
        out_shape=[shp, shp, shp],
        compiler_params=_params("arbitrary"),
        name="rope_tables",
    )(pos, invf_lane)


def _qk_norm_rope(z, g_ref, bd_ref, c_ref, s1_ref, s2_ref, scale):
    tn = z.shape[1]
    zz = z * z
    hi = zz.astype(BF16)
    lo = (zz - hi.astype(F32)).astype(BF16)
    bd = bd_ref[...]
    parts = []
    for c in range(tn // MXU_DIM):
        sl = slice(c * MXU_DIM, (c + 1) * MXU_DIM)
        parts.append(jnp.dot(hi[:, sl], bd, preferred_element_type=F32)
                     + jnp.dot(lo[:, sl], bd, preferred_element_type=F32))
    ss = jnp.concatenate(parts, axis=1) if len(parts) > 1 else parts[0]
    y = z * lax.rsqrt(ss * (1.0 / D_SUB) + EPS) * g_ref[...]
    cos, s1, s2 = c_ref[...], s1_ref[...], s2_ref[...]
    outs = []
    for c in range(tn // LANES):
        ys = y[:, c * LANES:(c + 1) * LANES]
        up = pltpu.roll(ys, LANES - ROT_DIM // 2, axis=1)
        dn = pltpu.roll(ys, ROT_DIM // 2, axis=1)
        outs.append(ys * cos + up * s1 + dn * s2)
    out = jnp.concatenate(outs, axis=1) if len(outs) > 1 else outs[0]
    if scale != 1.0:
        out = out * scale
    return out


def _mm_fullk_kernel(*refs, n_lhs, epilogue, scale, layouts):
    x_refs = refs[:n_lhs]
    w_ref = refs[n_lhs]
    extra = refs[n_lhs + 1:-len(layouts)]
    out_refs = refs[-len(layouts):]
    kk = w_ref.shape[0] // n_lhs
    acc = None
    for i, x_ref in enumerate(x_refs):
        part = jnp.dot(x_ref[...], w_ref[i * kk:(i + 1) * kk, :], preferred_element_type=F32)
        acc = part if acc is None else acc + part
    if epilogue == "relu2":
        a = jnp.maximum(acc, 0.0)
        acc = a * a
    elif epilogue == "resid":
        acc = extra[0][...] + acc
    elif epilogue == "qk":
        acc = _qk_norm_rope(acc, *extra, scale)
    for layout, o_ref in zip(layouts, out_refs):
        if layout == "rows":
            o_ref[...] = acc.astype(o_ref.dtype)
        else:
            for hh in range(o_ref.shape[0]):
                o_ref[hh] = acc[:, hh * D_V:(hh + 1) * D_V].astype(o_ref.dtype)


def mm_fullk(xs, w, col_off, n_cols, *, tm, tn, out_dtype, epilogue=None, extra=(), extra_specs=(),
             scale=1.0, layouts=("rows",), name="mm"):
    m = xs[0].shape[0]
    k_total = w.shape[0]
    off = col_off // tn
    x_specs = [pl.BlockSpec((tm, x.shape[1]), lambda n, i: (i, 0)) for x in xs]
    w_spec = pl.BlockSpec((k_total, tn), lambda n, i: (0, n + off))
    out_specs, out_shape = [], []
    for layout in layouts:
        if layout == "rows":
            out_specs.append(pl.BlockSpec((tm, tn), lambda n, i: (i, n)))
            out_shape.append(jax.ShapeDtypeStruct((m, n_cols), out_dtype))
        else:
            out_specs.append(pl.BlockSpec((tn // D_V, tm, D_V), lambda n, i: (n, i, 0)))
            out_shape.append(jax.ShapeDtypeStruct((n_cols // D_V, m, D_V), BF16))
    outs = pl.pallas_call(
        functools.partial(_mm_fullk_kernel, n_lhs=len(xs), epilogue=epilogue, scale=scale,
                          layouts=layouts),
        grid=(n_cols // tn, m // tm),
        in_specs=x_specs + [w_spec] + list(extra_specs),
        out_specs=out_specs,
        out_shape=out_shape,
        compiler_params=_params("arbitrary", "arbitrary"),
        name=name,
    )(*xs, w, *extra)
    return outs[0] if len(layouts) == 1 else outs


def _mm_ktiled_kernel(x_ref, w_ref, r_ref, o_ref):
    @pl.when(pl.program_id(2) == 0)
    def _():
        o_ref[...] = r_ref[...] + jnp.dot(x_ref[...], w_ref[...], preferred_element_type=F32)

    @pl.when(pl.program_id(2) > 0)
    def _():
        o_ref[...] += jnp.dot(x_ref[...], w_ref[...], preferred_element_type=F32)


def mm_ktiled_resid(x, w, resid, *, tm, tn, tk, name="mm_ktiled"):
    m, kdim = x.shape
    n = w.shape[1]
    nk = kdim // tk
    return pl.pallas_call(
        _mm_ktiled_kernel,
        grid=(m // tm, n // tn, nk),
        in_specs=[pl.BlockSpec((tm, tk), lambda i, j, k: (i, k)),
                  pl.BlockSpec((tk, tn), lambda i, j, k: (k, j)),
                  pl.BlockSpec((tm, tn), lambda i, j, k: (i, j))],
        out_specs=pl.BlockSpec((tm, tn), lambda i, j, k: (i, j)),
        out_shape=jax.ShapeDtypeStruct((m, n), F32),
        compiler_params=_params("arbitrary", "arbitrary", "arbitrary"),
        name=name,
    )(x, w, resid)


def _diff_lambda(lq1_ref, lk1_ref, lq2_ref, lk2_ref, lam_init):
    a = jnp.sum(lq1_ref[...] * lk1_ref[...], axis=-1, keepdims=True)
    b = jnp.sum(lq2_ref[...] * lk2_ref[...], axis=-1, keepdims=True)
    return jnp.exp(a) - jnp.exp(b) + lam_init


def _flash_kernel(q_ref, k_ref, v_ref, lq1_ref, lk1_ref, lq2_ref, lk2_ref, sg_ref, *rest,
                  n_cast, tq, tk, lam_init):
    cast_in, o_ref, cast_out = rest[:n_cast], rest[n_cast], rest[n_cast + 1:2 * n_cast + 1]
    vt_ref, m_ref, acc_ref, st_ref = rest[2 * n_cast + 1:]
    qi = pl.program_id(1)
    n_chunks = vt_ref.shape[0]

    for w_ref, wb_ref in zip(cast_in, cast_out):
        wb_ref[...] = w_ref[...].astype(BF16)

    @pl.when(qi == 0)
    def _():
        vt_ref[:, D_V:, :] = jnp.ones((n_chunks, ONES_ROWS, tk), BF16)
        for c in range(n_chunks):
            vt_ref[c, :D_V, :] = v_ref[c * tk:(c + 1) * tk, :].astype(F32).T.astype(BF16)

    q = q_ref[...].astype(F32)
    lane = lax.broadcasted_iota(jnp.int32, q.shape, 1)
    q2 = jnp.concatenate([jnp.where(lane < D_SUB, q, 0.0),
                          jnp.where(lane >= D_SUB, q, 0.0)], axis=0).astype(BF16)

    m_ref[...] = jnp.full_like(m_ref, NEG_INF)
    acc_ref[...] = jnp.zeros_like(acc_ref)

    def scores(j, slot):
        start = pl.multiple_of(j * tk, tk)
        kc = k_ref[pl.ds(start, tk), :]
        st_ref[slot] = lax.dot_general(kc, q2, (((1,), (1,)), ((), ())),
                                       preferred_element_type=F32)

    def consume(j, slot, masked):
        st = st_ref[slot]
        if masked:
            key = lax.broadcasted_iota(jnp.int32, st.shape, 0) + j * tk
            qpos = (lax.broadcasted_iota(jnp.int32, st.shape, 1) & (tq - 1)) + qi * tq
            st = jnp.where(key <= qpos, st, NEG_INF)
        m_prev = m_ref[...]
        m_new = jnp.maximum(m_prev, jnp.max(st, axis=0, keepdims=True))
        alpha = jnp.exp(m_prev - m_new)
        pt = jnp.exp(st - m_new)
        pv = jnp.dot(vt_ref[j], pt.astype(BF16), preferred_element_type=F32)
        acc_ref[...] = alpha * acc_ref[...] + pv
        m_ref[...] = m_new

    n_full = (qi * tq) // tk
    scores(0, 0)

    def body(jj, carry):
        scores(2 * jj + 1, 1)
        consume(2 * jj, 0, False)
        scores(2 * jj + 2, 0)
        consume(2 * jj + 1, 1, False)
        return carry

    lax.fori_loop(0, n_full // 2, body, 0)

    @pl.when(n_full % 2 == 1)
    def _():
        scores(n_full, 1)
        consume(n_full - 1, 0, False)
        consume(n_full, 1, True)

    @pl.when(n_full % 2 == 0)
    def _():
        consume(n_full, 0, True)

    acc = acc_ref[:D_V, :]
    l = acc_ref[D_V:D_V + 1, :]
    lam = _diff_lambda(lq1_ref, lk1_ref, lq2_ref, lk2_ref, lam_init)
    o = acc[:, :tq] / l[:, :tq] - lam * (acc[:, tq:] / l[:, tq:])
    ms = jnp.mean(o * o, axis=0, keepdims=True)
    y = o * lax.rsqrt(ms + EPS) * sg_ref[...] * (1.0 - lam_init)
    o_ref[...] = y.T.astype(o_ref.dtype)


def flash_diff_attention(q, k, v, lam_vecs, subln_g, lam_init, cast_weights, *, tq, tk):
    n_heads, s, _ = q.shape
    width = n_heads * D_V
    n_q = s // tq
    assert tk % tq == 0 and s % tk == 0
    vec = pl.BlockSpec((1, D_SUB), lambda h, i: (0, 0))
    cast_specs = []
    for w in cast_weights:
        assert w.shape[0] % n_heads == 0 and w.shape[1] % n_q == 0
        cast_specs.append(pl.BlockSpec((w.shape[0] // n_heads, w.shape[1] // n_q), lambda h, i: (h, i)))
    outs = pl.pallas_call(
        functools.partial(_flash_kernel, n_cast=len(cast_weights), tq=tq, tk=tk, lam_init=lam_init),
        grid=(n_heads, n_q),
        in_specs=[pl.BlockSpec((None, tq, D_V), lambda h, i: (h, i, 0)),
                  pl.BlockSpec((None, s, D_V), lambda h, i: (h, 0, 0)),
                  pl.BlockSpec((None, s, D_V), lambda h, i: (h, 0, 0)),
                  vec, vec, vec, vec,
                  pl.BlockSpec((D_V, 1), lambda h, i: (0, 0))] + cast_specs,
        out_specs=[pl.BlockSpec((tq, D_V), lambda h, i: (i, h))] + cast_specs,
        out_shape=[jax.ShapeDtypeStruct((s, width), BF16)]
                  + [jax.ShapeDtypeStruct(w.shape, BF16) for w in cast_weights],
        scratch_shapes=[pltpu.VMEM((s // tk, D_V + ONES_ROWS, tk), BF16),
                        pltpu.VMEM((1, 2 * tq), F32),
                        pltpu.VMEM((D_V + ONES_ROWS, 2 * tq), F32),
                        pltpu.VMEM((2, tk, 2 * tq), F32)],
        compiler_params=_params("arbitrary", "arbitrary"),
        name="flash_diff_attention",
    )(q, k, v, *lam_vecs, subln_g.reshape(D_V, 1), *cast_weights)
    return outs[0], outs[1:]


def _paged_update(p_idx, n_steps, q_ref, kn_ref, vn_ref, k_refs, v_refs, lam_refs, sg_ref, o_ref,
                  scratch, host_work, *, n_heads, lam_init):
    lq1_ref, lk1_ref, lq2_ref, lk2_ref = lam_refs
    q2_ref, bias_ref, m_ref, l_ref, acc_ref = scratch

    @pl.when(p_idx == 0)
    def _():
        q = q_ref[...].astype(F32)
        lane = lax.broadcasted_iota(jnp.int32, q.shape, 1)
        q2_ref[...] = jnp.concatenate([jnp.where(lane < D_SUB, q, 0.0),
                                       jnp.where(lane >= D_SUB, q, 0.0)], axis=0).astype(BF16)
        row = lax.broadcasted_iota(jnp.int32, bias_ref.shape, 0)
        col = lax.broadcasted_iota(jnp.int32, bias_ref.shape, 1)
        same_head = (col & (n_heads - 1)) == (row & (n_heads - 1))
        bias_ref[...] = jnp.where(same_head, 0.0, NEG_INF)
        m_ref[...] = jnp.full_like(m_ref, NEG_INF)
        l_ref[...] = jnp.zeros_like(l_ref)
        acc_ref[...] = jnp.zeros_like(acc_ref)

    q2 = q2_ref[...]
    bias = bias_ref[...]
    s_parts = [lax.dot_general(q2, k_ref[...].astype(BF16), (((1,), (1,)), ((), ())),
                               preferred_element_type=F32) + bias for k_ref in k_refs]
    host_work()
    m_prev = m_ref[...]
    m_new = m_prev
    for s in s_parts:
        m_new = jnp.maximum(m_new, jnp.max(s, axis=1, keepdims=True))
    alpha = jnp.exp(m_prev - m_new)
    l_new = alpha * l_ref[...]
    acc = alpha * acc_ref[...]
    for s, v_ref in zip(s_parts, v_refs):
        p = jnp.exp(s - m_new)
        l_new = l_new + jnp.sum(p, axis=1, keepdims=True)
        acc = acc + jnp.dot(p.astype(BF16), v_ref[...].astype(BF16), preferred_element_type=F32)
    l_ref[...] = l_new
    acc_ref[...] = acc
    m_ref[...] = m_new

    @pl.when(p_idx == n_steps - 1)
    def _():
        kn2 = jnp.concatenate([kn_ref[...], kn_ref[...]], axis=0)
        vn2 = jnp.concatenate([vn_ref[...], vn_ref[...]], axis=0)
        s_new = jnp.sum(q2_ref[...].astype(F32) * kn2, axis=1, keepdims=True)
        m_old = m_ref[...]
        m_fin = jnp.maximum(m_old, s_new)
        a = jnp.exp(m_old - m_fin)
        p_new = jnp.exp(s_new - m_fin)
        l = a * l_ref[...] + p_new
        acc_f = a * acc_ref[...] + p_new * vn2
        lam = _diff_lambda(lq1_ref, lk1_ref, lq2_ref, lk2_ref, lam_init)
        o = acc_f[:n_heads] / l[:n_heads] - lam * (acc_f[n_heads:] / l[n_heads:])
        ms = jnp.mean(o * o, axis=1, keepdims=True)
        y = o * lax.rsqrt(ms + EPS) * sg_ref[...] * (1.0 - lam_init)
        o_ref[...] = y.astype(o_ref.dtype)


def _mlp_paged_kernel(pt_ref, *refs, host, grid_dims, pp, steps_per_seq, n_heads, lam_init):
    del pt_ref
    n_host = 2 if host == "up" else 3
    host_in, rest = refs[:n_host], refs[n_host:]
    q_ref, kn_ref, vn_ref = rest[:3]
    k_refs, v_refs = rest[3:3 + pp], rest[3 + pp:3 + 2 * pp]
    lam_refs = rest[3 + 2 * pp:7 + 2 * pp]
    sg_ref, o_ref, attn_ref = rest[7 + 2 * pp:10 + 2 * pp]
    scratch = rest[10 + 2 * pp:]

    def host_work():
        if host == "up":
            part = jnp.dot(host_in[0][...], host_in[1][...], preferred_element_type=F32)
            a = jnp.maximum(part, 0.0)
            o_ref[...] = (a * a).astype(o_ref.dtype)
        else:
            @pl.when(pl.program_id(2) == 0)
            def _():
                o_ref[...] = host_in[2][...] + jnp.dot(host_in[0][...], host_in[1][...],
                                                      preferred_element_type=F32)

            @pl.when(pl.program_id(2) > 0)
            def _():
                o_ref[...] += jnp.dot(host_in[0][...], host_in[1][...], preferred_element_type=F32)

    step = pl.program_id(0)
    for axis in range(1, len(grid_dims)):
        step = step * grid_dims[axis] + pl.program_id(axis)
    _paged_update(lax.rem(step, steps_per_seq), steps_per_seq, q_ref, kn_ref, vn_ref, k_refs, v_refs,
                  lam_refs, sg_ref, attn_ref, scratch, host_work, n_heads=n_heads, lam_init=lam_init)


def mlp_matmul_with_paged_attention(host, x, w, resid, sample, seq_lo, n_seq, *, tm, tn, tk,
                                    pages_per_step):
    m, kdim = x.shape
    n = w.shape[1]
    pp = pages_per_step
    n_heads = sample["n_heads"]
    page_table = sample["page_table"]
    steps_per_seq = page_table.shape[1] // pp
    rows = 2 * n_heads
    page_rows = PAGE_SIZE * n_heads
    if host == "up":
        grid = (n // tn, m // tm)
        assert tk == kdim
        host_specs = [pl.BlockSpec((tm, kdim), lambda j, i, pt: (i, 0)),
                      pl.BlockSpec((kdim, tn), lambda j, i, pt: (0, j))]
        host_out = pl.BlockSpec((tm, tn), lambda j, i, pt: (i, j))
        host_args, out_dtype = (x, w), BF16
    else:
        grid = (m // tm, n // tn, kdim // tk)
        host_specs = [pl.BlockSpec((tm, tk), lambda i, j, k, pt: (i, k)),
                      pl.BlockSpec((tk, tn), lambda i, j, k, pt: (k, j)),
                      pl.BlockSpec((tm, tn), lambda i, j, k, pt: (i, j))]
        host_out = pl.BlockSpec((tm, tn), lambda i, j, k, pt: (i, j))
        host_args, out_dtype = (x, w, resid), F32
    n_steps = math.prod(grid)
    assert n_steps == n_seq * steps_per_seq, (grid, n_seq, steps_per_seq)

    def linear(ids):
        step = ids[0]
        for axis in range(1, len(grid)):
            step = step * grid[axis] + ids[axis]
        return step

    def seq_map(*args):
        return (seq_lo + linear(args[:-1]) // steps_per_seq, 0, 0)

    def page_map(c):
        def index(*args):
            step, pt = linear(args[:-1]), args[-1]
            seq = seq_lo + step // steps_per_seq
            return (sample["page_base"] + pt[seq, (step % steps_per_seq) * pp + c], 0, 0)
        return index

    def const_map(*args):
        return (0, 0)

    seq_spec = pl.BlockSpec((None, n_heads, D_V), seq_map)
    page_specs = [pl.BlockSpec((None, page_rows, D_V), page_map(c)) for c in range(pp)]
    vec = pl.BlockSpec((1, D_SUB), const_map)
    out, attn = pl.pallas_call(
        functools.partial(_mlp_paged_kernel, host=host, grid_dims=grid, pp=pp,
                          steps_per_seq=steps_per_seq, n_heads=n_heads, lam_init=sample["lam_init"]),
        grid_spec=pltpu.PrefetchScalarGridSpec(
            num_scalar_prefetch=1,
            grid=grid,
            in_specs=host_specs + [seq_spec, seq_spec, seq_spec] + page_specs + page_specs
                     + [vec, vec, vec, vec, pl.BlockSpec((1, D_V), const_map)],
            out_specs=[host_out,
                       pl.BlockSpec((None, n_heads, D_V),
                                    lambda *args: (linear(args[:-1]) // steps_per_seq, 0, 0))],
            scratch_shapes=[pltpu.VMEM((rows, D_V), BF16),
                            pltpu.VMEM((rows, page_rows), F32),
                            pltpu.VMEM((rows, 1), F32), pltpu.VMEM((rows, 1), F32),
                            pltpu.VMEM((rows, D_V), F32)]),
        out_shape=[jax.ShapeDtypeStruct((m, n), out_dtype),
                   jax.ShapeDtypeStruct((n_seq, n_heads, D_V), BF16)],
        compiler_params=_params(*(["arbitrary"] * len(grid))),
        name="mlp_" + host + "_paged_attention",
    )(page_table, *host_args, sample["q"], sample["k_new"], sample["v_new"],
      *([sample["cache_k"]] * pp), *([sample["cache_v"]] * pp), *sample["lam_vecs"],
      sample["subln_g"].reshape(1, D_V))
    return out, attn.reshape(n_seq, n_heads * D_V)


HALO = POOL_BUF + 1


def _pool_prompt_kernel(u_ref, halo_ref, w_ref, sc_ref, o_ref, ext_ref, *, tm, group):
    i = pl.program_id(0)
    halo = halo_ref[...]
    ext_ref[0:HALO, :] = jnp.where(i > 0, halo, jnp.zeros_like(halo))
    ext_ref[HALO:, :] = u_ref[...]
    pos = i * tm + lax.broadcasted_iota(jnp.int32, (tm, 1), 0)
    for g, win in enumerate(POOL_WINDOWS):
        cols = slice(g * group, (g + 1) * group)
        cur = u_ref[:, cols]
        tot = cur
        for k in range(1, win):
            tot = tot + ext_ref[HALO - k:HALO - k + tm, cols]
        cnt = jnp.minimum(win, pos + 1).astype(F32)
        d = tot / cnt - cur
        y = jnp.dot(d.astype(BF16), w_ref[g], preferred_element_type=F32) * sc_ref[:, cols]
        o_ref[:, cols] = y.astype(o_ref.dtype)


def pool_mix_prompt(u, pool_w, pool_scale, *, tm):
    s, width = u.shape
    group = width // len(POOL_WINDOWS)
    ratio = tm // HALO
    return pl.pallas_call(
        functools.partial(_pool_prompt_kernel, tm=tm, group=group),
        grid=(s // tm,),
        in_specs=[pl.BlockSpec((tm, width), lambda i: (i, 0)),
                  pl.BlockSpec((HALO, width), lambda i: (jnp.maximum(i * ratio - 1, 0), 0)),
                  pl.BlockSpec(pool_w.shape, lambda i: (0, 0, 0)),
                  pl.BlockSpec((1, width), lambda i: (0, 0))],
        out_specs=pl.BlockSpec((tm, width), lambda i: (i, 0)),
        out_shape=jax.ShapeDtypeStruct((s, width), BF16),
        scratch_shapes=[pltpu.VMEM((tm + HALO, width), F32)],
        compiler_params=_params("arbitrary"),
        name="pool_mix_prompt",
    )(u, u, pool_w, pool_scale.reshape(1, width))


def _pool_sample_kernel(st_ref, us_ref, w_ref, sc_ref, o_ref, *, group, first_pos):
    cur_all = us_ref[...]
    for g, win in enumerate(POOL_WINDOWS):
        cols = slice(g * group, (g + 1) * group)
        cur = cur_all[:, cols]
        tot = cur
        for k in range(1, win):
            tot = tot + st_ref[POOL_BUF - k, :, cols]
        cnt = float(min(win, first_pos + 1))
        d = tot / cnt - cur
        y = jnp.dot(d.astype(BF16), w_ref[g], preferred_element_type=F32) * sc_ref[:, cols]
        o_ref[:, cols] = y.astype(o_ref.dtype)


def pool_mix_sample(state_t, us, pool_w, pool_scale, first_pos):
    b, width = us.shape
    group = width // len(POOL_WINDOWS)
    return pl.pallas_call(
        functools.partial(_pool_sample_kernel, group=group, first_pos=first_pos),
        grid=(1,),
        in_specs=[pl.BlockSpec(state_t.shape, lambda i: (0, 0, 0)),
                  pl.BlockSpec((b, width), lambda i: (0, 0)),
                  pl.BlockSpec(pool_w.shape, lambda i: (0, 0, 0)),
                  pl.BlockSpec((1, width), lambda i: (0, 0))],
        out_specs=pl.BlockSpec((b, width), lambda i: (0, 0)),
        out_shape=jax.ShapeDtypeStruct((b, width), BF16),
        compiler_params=_params("arbitrary"),
        name="pool_mix_sample",
    )(state_t, us, pool_w, pool_scale.reshape(1, width))


def _block_diag_ones():
    r = jnp.arange(MXU_DIM) // D_SUB
    return (r[:, None] == r[None, :]).astype(BF16)


def _tile_lanes(vec, n):
    return jnp.tile(vec, n // vec.shape[0]).reshape(1, n)


def project(x, pos, wts, *, tm, tn, head_major=False):
    q_layouts = ("heads",) if head_major else ("rows",)
    kv_layouts = ("rows", "heads") if head_major else ("rows",)
    d = x.shape[1]
    attn_w = d // 2
    h = rmsnorm(x, wts["norm1_g"], min(tm, 256))
    cos, s1, s2 = rope_tables(pos, tm)
    tab_spec = pl.BlockSpec((tm, LANES), lambda n, i: (i, 0))
    vec_spec = pl.BlockSpec((1, tn), lambda n, i: (0, 0))
    bd_spec = pl.BlockSpec((MXU_DIM, MXU_DIM), lambda n, i: (0, 0))
    specs = (vec_spec, bd_spec, tab_spec, tab_spec, tab_spec)
    bd = _block_diag_ones()
    q = mm_fullk([h], wts["w_in"], 0, attn_w, tm=tm, tn=tn, out_dtype=BF16, epilogue="qk",
                 extra=(_tile_lanes(wts["q_norm_g"], tn), bd, cos, s1, s2), extra_specs=specs,
                 scale=D_SUB ** -0.5, layouts=q_layouts, name="proj_q")
    k = mm_fullk([h], wts["w_in"], attn_w, attn_w, tm=tm, tn=tn, out_dtype=F32, epilogue="qk",
                 extra=(_tile_lanes(wts["k_norm_g"], tn), bd, cos, s1, s2), extra_specs=specs,
                 layouts=kv_layouts, name="proj_k")
    v = mm_fullk([h], wts["w_in"], 2 * attn_w, attn_w, tm=tm, tn=tn, out_dtype=F32,
                 layouts=kv_layouts, name="proj_v")
    u = mm_fullk([h], wts["w_in"], 3 * attn_w, d - attn_w, tm=tm, tn=tn, out_dtype=F32, name="proj_u")
    return q, k, v, u


def finish(x, attn, pool, wts, *, tm, tn, tn_up, tm_down, tn_down, tk_down, sample=None,
           pages_per_step=None):
    d = x.shape[1]
    x1 = mm_fullk([attn, pool], wts["w_out"], 0, d, tm=tm, tn=tn, out_dtype=F32, epilogue="resid",
                  extra=(x,), extra_specs=(pl.BlockSpec((tm, tn), lambda n, i: (i, n)),),
                  name="out_proj")
    h2 = rmsnorm(x1, wts["norm2_g"], min(tm, 256))
    if sample is None:
        a = mm_fullk([h2], wts["w_up"], 0, wts["w_up"].shape[1], tm=tm, tn=tn_up, out_dtype=BF16,
                     epilogue="relu2", name="mlp_up")
        y = mm_ktiled_resid(a, wts["w_down"], x1, tm=tm_down, tn=tn_down, tk=tk_down, name="mlp_down")
        return y, None
    n_seq = sample["q"].shape[0]
    n_lo = n_seq // 2
    a, attn_lo = mlp_matmul_with_paged_attention(
        "up", h2, wts["w_up"], None, sample, 0, n_lo, tm=tm, tn=tn_up, tk=d,
        pages_per_step=pages_per_step)
    y, attn_hi = mlp_matmul_with_paged_attention(
        "down", a, wts["w_down"], x1, sample, n_lo, n_seq - n_lo, tm=tm_down, tn=tn_down, tk=tk_down,
        pages_per_step=pages_per_step)
    return y, jnp.concatenate([attn_lo, attn_hi], axis=0)


def kernel(x_prompt, x_sample, cache_k, cache_v, state_pool, page_table, norm1_g, w_in, q_norm_g,
           k_norm_g, lambda_q1, lambda_k1, lambda_q2, lambda_k2, subln_g, pool_w, pool_scale, w_out,
           norm2_g, w_up, w_down):
    batch, seq, d_model = x_prompt.shape
    dec_batch, dec_seq, _ = x_sample.shape
    assert batch == 1 and dec_seq == 1
    depth = w_in.shape[0]
    n_phys = cache_k.shape[1]
    n_heads = cache_k.shape[3]
    attn_w = n_heads * D_V
    pool_width = d_model - attn_w
    past_len = page_table.shape[1] * PAGE_SIZE

    xp = x_prompt.reshape(seq, d_model)
    xs = x_sample.reshape(dec_batch, d_model)
    pos_p = jnp.arange(seq, dtype=F32).reshape(seq, 1)
    pos_s = jnp.full((dec_batch, 1), past_len, F32)

    outs = [[] for _ in range(6)]
    for l in range(depth):
        lam_init = 0.8 - 0.6 * math.exp(-0.3 * l)
        wts = {
            "norm1_g": norm1_g[l], "q_norm_g": q_norm_g[l], "k_norm_g": k_norm_g[l],
            "norm2_g": norm2_g[l],
            "w_in": w_in[l].astype(BF16),
        }
        pool_w_l = pool_w[l].astype(BF16)
        lam_vecs = [v[l].reshape(1, D_SUB) for v in (lambda_q1, lambda_k1, lambda_q2, lambda_k2)]

        q_hm, (k, k_hm), (v, v_hm), u = project(xp, pos_p, wts, tm=512, tn=1024, head_major=True)
        qs, kn, vn, us = project(xs, pos_s, wts, tm=dec_batch, tn=512)
        to_heads = lambda a: a.reshape(dec_batch, n_heads, D_V)
        sample = {
            "n_heads": n_heads, "page_table": page_table, "page_base": l * n_phys,
            "q": to_heads(qs), "k_new": to_heads(kn), "v_new": to_heads(vn),
            "cache_k": cache_k.reshape(-1, PAGE_SIZE * n_heads, D_V),
            "cache_v": cache_v.reshape(-1, PAGE_SIZE * n_heads, D_V),
            "lam_vecs": lam_vecs, "subln_g": subln_g[l], "lam_init": lam_init,
        }

        attn, (wts["w_out"], wts["w_up"], wts["w_down"]) = flash_diff_attention(
            q_hm, k_hm, v_hm, lam_vecs, subln_g[l], lam_init, (w_out[l], w_up[l], w_down[l]),
            tq=512, tk=512)
        pool = pool_mix_prompt(u, pool_w_l, pool_scale[l], tm=256)
        xp, attn_s = finish(xp, attn, pool, wts, tm=512, tn=1024, tn_up=1024, tm_down=1024,
                            tn_down=1024, tk_down=2048, sample=sample, pages_per_step=4)
        outs[0].append(k.reshape(batch, seq, n_heads, D_V))
        outs[1].append(v.reshape(batch, seq, n_heads, D_V))
        outs[2].append(u[seq - POOL_BUF:].reshape(batch, POOL_BUF, pool_width))

        state_l = state_pool[l]
        pool_s = pool_mix_sample(jnp.swapaxes(state_l, 0, 1), us, pool_w_l, pool_scale[l], past_len)
        xs, _ = finish(xs, attn_s, pool_s, wts, tm=dec_batch, tn=1024, tn_up=1024, tm_down=dec_batch,
                       tn_down=2048, tk_down=2048)
        outs[3].append(kn.reshape(dec_batch, dec_seq, n_heads, D_V))
        outs[4].append(vn.reshape(dec_batch, dec_seq, n_heads, D_V))
        outs[5].append(jnp.concatenate([state_l[:, 1:], us[:, None, :]], axis=1))

    k_prompt, v_prompt, pool_prompt, k_sample, v_sample, pool_sample = (jnp.stack(o) for o in outs)
    return (xp.reshape(batch, seq, d_model), xs.reshape(dec_batch, dec_seq, d_model),
            k_prompt, v_prompt, pool_prompt, k_sample, v_sample, pool_sample)
```

```python
import functools
import math

import jax
import jax.numpy as jnp
from jax import lax
from jax.experimental import pallas as pl
from jax.experimental.pallas import tpu as pltpu

PAGE_SIZE = 128
D_V = 128
D_SUB = D_V // 2
ROT_DIM = D_SUB // 4
ROPE_THETA = 500000.0
POOL_WINDOWS = (2, 4, 8, 16)
POOL_BUF = max(POOL_WINDOWS) - 1
EPS = 1e-6
NEG_INF = -1e30

LANES = 128
MXU_DIM = 256
ONES_ROWS = 16
VMEM_BYTES_V7X = 64 * 1024 * 1024
VMEM_LIMIT_BYTES = VMEM_BYTES_V7X - 8 * 1024 * 1024


def _tile_plan(rows, is_prompt):
    if is_prompt:
        return dict(tm=512, tn_proj=1024, tn=1024, tn_up=1024, tm_down=1024, tn_down=1024,
                    tk_down=2048, tq=512, tk=512, tm_pool=256, pages_per_step=4)
    return dict(tm=rows, tn_proj=512, tn=1024, tn_up=1024, tm_down=rows, tn_down=2048,
                tk_down=2048)

F32 = jnp.float32
BF16 = jnp.bfloat16


def _params(*sem):
    return pltpu.CompilerParams(dimension_semantics=sem, vmem_limit_bytes=VMEM_LIMIT_BYTES)


def _rmsnorm_kernel(x_ref, g_ref, o_ref):
    x = x_ref[...]
    ms = jnp.mean(x * x, axis=-1, keepdims=True)
    o_ref[...] = (x * lax.rsqrt(ms + EPS) * g_ref[...]).astype(o_ref.dtype)


def rmsnorm(x, g, tm):
    m, d = x.shape
    return pl.pallas_call(
        _rmsnorm_kernel,
        grid=(m // tm,),
        in_specs=[pl.BlockSpec((tm, d), lambda i: (i, 0)),
                  pl.BlockSpec((1, d), lambda i: (0, 0))],
        out_specs=pl.BlockSpec((tm, d), lambda i: (i, 0)),
        out_shape=jax.ShapeDtypeStruct((m, d), BF16),
        compiler_params=_params("arbitrary"),
        name="rmsnorm",
    )(x, g.reshape(1, d))


def _rope_table_kernel(pos_ref, invf_ref, c_ref, s1_ref, s2_ref):
    ang = pos_ref[...] * invf_ref[...]
    c = jnp.cos(ang)
    s = jnp.sin(ang)
    r = lax.broadcasted_iota(jnp.int32, ang.shape, 1) & (D_SUB - 1)
    c_ref[...] = c
    s1_ref[...] = jnp.where(r < ROT_DIM // 2, -s, 0.0)
    s2_ref[...] = jnp.where((r >= ROT_DIM // 2) & (r < ROT_DIM), s, 0.0)


def rope_tables(pos, tm):
    m = pos.shape[0]
    inv_freq = ROPE_THETA ** (-jnp.arange(0, ROT_DIM, 2, dtype=F32) / ROT_DIM)
    sub = jnp.concatenate([inv_freq, inv_freq, jnp.zeros((D_SUB - ROT_DIM,), F32)])
    invf_lane = jnp.tile(sub, LANES // D_SUB).reshape(1, LANES)
    spec = pl.BlockSpec((tm, LANES), lambda i: (i, 0))
    shp = jax.ShapeDtypeStruct((m, LANES), F32)
    return pl.pallas_call(
        _rope_table_kernel,
        grid=(m // tm,),
        in_specs=[pl.BlockSpec((tm, 1), lambda i: (i, 0)),
                  pl.BlockSpec((1, LANES), lambda i: (0, 0))],
        out_specs=[spec, spec, spec],
        out_shape=[shp, shp, shp],
        compiler_params=_params("arbitrary"),
        name="rope_tables",
    )(pos, invf_lane)


def _qk_norm_rope(z, g_ref, bd_ref, c_ref, s1_ref, s2_ref, scale):
    tn = z.shape[1]
    zz = z * z
    hi = zz.astype(BF16)
    lo = (zz - hi.astype(F32)).astype(BF16)
    bd = bd_ref[...]
    parts = []
    for c in range(tn // MXU_DIM):
        sl = slice(c * MXU_DIM, (c + 1) * MXU_DIM)
        parts.append(jnp.dot(hi[:, sl], bd, preferred_element_type=F32)
                     + jnp.dot(lo[:, sl], bd, preferred_element_type=F32))
    ss = jnp.concatenate(parts, axis=1) if len(parts) > 1 else parts[0]
    y = z * lax.rsqrt(ss * (1.0 / D_SUB) + EPS) * g_ref[...]
    cos, s1, s2 = c_ref[...], s1_ref[...], s2_ref[...]
    outs = []
    for c in range(tn // LANES):
        ys = y[:, c * LANES:(c + 1) * LANES]
        up = pltpu.roll(ys, LANES - ROT_DIM // 2, axis=1)
        dn = pltpu.roll(ys, ROT_DIM // 2, axis=1)
        outs.append(ys * cos + up * s1 + dn * s2)
    out = jnp.concatenate(outs, axis=1) if len(outs) > 1 else outs[0]
    if scale != 1.0:
        out = out * scale
    return out


def _mm_fullk_kernel(*refs, n_lhs, epilogue, scale, layouts):
    x_refs = refs[:n_lhs]
    w_ref = refs[n_lhs]
    extra = refs[n_lhs + 1:-len(layouts)]
    out_refs = refs[-len(layouts):]
    kk = w_ref.shape[0] // n_lhs
    acc = None
    for i, x_ref in enumerate(x_refs):
        part = jnp.dot(x_ref[...], w_ref[i * kk:(i + 1) * kk, :], preferred_element_type=F32)
        acc = part if acc is None else acc + part
    if epilogue == "relu2":
        a = jnp.maximum(acc, 0.0)
        acc = a * a
    elif epilogue == "resid":
        acc = extra[0][...] + acc
    elif epilogue == "qk":
        acc = _qk_norm_rope(acc, *extra, scale)
    for layout, o_ref in zip(layouts, out_refs):
        if layout == "rows":
            o_ref[...] = acc.astype(o_ref.dtype)
        else:
            for hh in range(o_ref.shape[0]):
                o_ref[hh] = acc[:, hh * D_V:(hh + 1) * D_V].astype(o_ref.dtype)


def mm_fullk(xs, w, col_off, n_cols, *, tm, tn, out_dtype, epilogue=None, extra=(), extra_specs=(),
             scale=1.0, layouts=("rows",), name="mm"):
    m = xs[0].shape[0]
    k_total = w.shape[0]
    off = col_off // tn
    x_specs = [pl.BlockSpec((tm, x.shape[1]), lambda n, i: (i, 0)) for x in xs]
    w_spec = pl.BlockSpec((k_total, tn), lambda n, i: (0, n + off))
    out_specs, out_shape = [], []
    for layout in layouts:
        if layout == "rows":
            out_specs.append(pl.BlockSpec((tm, tn), lambda n, i: (i, n)))
            out_shape.append(jax.ShapeDtypeStruct((m, n_cols), out_dtype))
        else:
            out_specs.append(pl.BlockSpec((tn // D_V, tm, D_V), lambda n, i: (n, i, 0)))
            out_shape.append(jax.ShapeDtypeStruct((n_cols // D_V, m, D_V), BF16))
    outs = pl.pallas_call(
        functools.partial(_mm_fullk_kernel, n_lhs=len(xs), epilogue=epilogue, scale=scale,
                          layouts=layouts),
        grid=(n_cols // tn, m // tm),
        in_specs=x_specs + [w_spec] + list(extra_specs),
        out_specs=out_specs,
        out_shape=out_shape,
        compiler_params=_params("arbitrary", "arbitrary"),
        name=name,
    )(*xs, w, *extra)
    return outs[0] if len(layouts) == 1 else outs


def _mm_ktiled_kernel(x_ref, w_ref, r_ref, o_ref):
    @pl.when(pl.program_id(2) == 0)
    def _():
        o_ref[...] = r_ref[...] + jnp.dot(x_ref[...], w_ref[...], preferred_element_type=F32)

    @pl.when(pl.program_id(2) > 0)
    def _():
        o_ref[...] += jnp.dot(x_ref[...], w_ref[...], preferred_element_type=F32)


def mm_ktiled_resid(x, w, resid, *, tm, tn, tk, name="mm_ktiled"):
    m, kdim = x.shape
    n = w.shape[1]
    nk = kdim // tk
    return pl.pallas_call(
        _mm_ktiled_kernel,
        grid=(m // tm, n // tn, nk),
        in_specs=[pl.BlockSpec((tm, tk), lambda i, j, k: (i, k)),
                  pl.BlockSpec((tk, tn), lambda i, j, k: (k, j)),
                  pl.BlockSpec((tm, tn), lambda i, j, k: (i, j))],
        out_specs=pl.BlockSpec((tm, tn), lambda i, j, k: (i, j)),
        out_shape=jax.ShapeDtypeStruct((m, n), F32),
        compiler_params=_params("arbitrary", "arbitrary", "arbitrary"),
        name=name,
    )(x, w, resid)


def _diff_lambda(lq1_ref, lk1_ref, lq2_ref, lk2_ref, lam_init):
    a = jnp.sum(lq1_ref[...] * lk1_ref[...], axis=-1, keepdims=True)
    b = jnp.sum(lq2_ref[...] * lk2_ref[...], axis=-1, keepdims=True)
    return jnp.exp(a) - jnp.exp(b) + lam_init


def _flash_kernel(q_ref, k_ref, v_ref, lq1_ref, lk1_ref, lq2_ref, lk2_ref, sg_ref, *rest,
                  n_cast, tq, tk, lam_init):
    cast_in, o_ref, cast_out = rest[:n_cast], rest[n_cast], rest[n_cast + 1:2 * n_cast + 1]
    vt_ref, m_ref, acc_ref, st_ref = rest[2 * n_cast + 1:]
    qi = pl.program_id(1)
    n_chunks = vt_ref.shape[0]

    for w_ref, wb_ref in zip(cast_in, cast_out):
        wb_ref[...] = w_ref[...].astype(BF16)

    @pl.when(qi == 0)
    def _():
        vt_ref[:, D_V:, :] = jnp.ones((n_chunks, ONES_ROWS, tk), BF16)
        for c in range(n_chunks):
            vt_ref[c, :D_V, :] = v_ref[c * tk:(c + 1) * tk, :].astype(F32).T.astype(BF16)

    q = q_ref[...].astype(F32)
    lane = lax.broadcasted_iota(jnp.int32, q.shape, 1)
    q2 = jnp.concatenate([jnp.where(lane < D_SUB, q, 0.0),
                          jnp.where(lane >= D_SUB, q, 0.0)], axis=0).astype(BF16)

    m_ref[...] = jnp.full_like(m_ref, NEG_INF)
    acc_ref[...] = jnp.zeros_like(acc_ref)

    def scores(j, slot):
        start = pl.multiple_of(j * tk, tk)
        kc = k_ref[pl.ds(start, tk), :]
        st_ref[slot] = lax.dot_general(kc, q2, (((1,), (1,)), ((), ())),
                                       preferred_element_type=F32)

    def consume(j, slot, masked):
        st = st_ref[slot]
        if masked:
            key = lax.broadcasted_iota(jnp.int32, st.shape, 0) + j * tk
            qpos = (lax.broadcasted_iota(jnp.int32, st.shape, 1) & (tq - 1)) + qi * tq
            st = jnp.where(key <= qpos, st, NEG_INF)
        m_prev = m_ref[...]
        m_new = jnp.maximum(m_prev, jnp.max(st, axis=0, keepdims=True))
        alpha = jnp.exp(m_prev - m_new)
        pt = jnp.exp(st - m_new)
        pv = jnp.dot(vt_ref[j], pt.astype(BF16), preferred_element_type=F32)
        acc_ref[...] = alpha * acc_ref[...] + pv
        m_ref[...] = m_new

    n_full = (qi * tq) // tk
    scores(0, 0)

    def body(jj, carry):
        scores(2 * jj + 1, 1)
        consume(2 * jj, 0, False)
        scores(2 * jj + 2, 0)
        consume(2 * jj + 1, 1, False)
        return carry

    lax.fori_loop(0, n_full // 2, body, 0)

    @pl.when(n_full % 2 == 1)
    def _():
        scores(n_full, 1)
        consume(n_full - 1, 0, False)
        consume(n_full, 1, True)

    @pl.when(n_full % 2 == 0)
    def _():
        consume(n_full, 0, True)

    acc = acc_ref[:D_V, :]
    l = acc_ref[D_V:D_V + 1, :]
    lam = _diff_lambda(lq1_ref, lk1_ref, lq2_ref, lk2_ref, lam_init)
    o = acc[:, :tq] / l[:, :tq] - lam * (acc[:, tq:] / l[:, tq:])
    ms = jnp.mean(o * o, axis=0, keepdims=True)
    y = o * lax.rsqrt(ms + EPS) * sg_ref[...] * (1.0 - lam_init)
    o_ref[...] = y.T.astype(o_ref.dtype)


def flash_diff_attention(q, k, v, lam_vecs, subln_g, lam_init, cast_weights, *, tq, tk):
    n_heads, s, _ = q.shape
    width = n_heads * D_V
    n_q = s // tq
    assert tk % tq == 0 and s % tk == 0
    vec = pl.BlockSpec((1, D_SUB), lambda h, i: (0, 0))
    cast_specs = []
    for w in cast_weights:
        assert w.shape[0] % n_heads == 0 and w.shape[1] % n_q == 0
        cast_specs.append(pl.BlockSpec((w.shape[0] // n_heads, w.shape[1] // n_q), lambda h, i: (h, i)))
    outs = pl.pallas_call(
        functools.partial(_flash_kernel, n_cast=len(cast_weights), tq=tq, tk=tk, lam_init=lam_init),
        grid=(n_heads, n_q),
        in_specs=[pl.BlockSpec((None, tq, D_V), lambda h, i: (h, i, 0)),
                  pl.BlockSpec((None, s, D_V), lambda h, i: (h, 0, 0)),
                  pl.BlockSpec((None, s, D_V), lambda h, i: (h, 0, 0)),
                  vec, vec, vec, vec,
                  pl.BlockSpec((D_V, 1), lambda h, i: (0, 0))] + cast_specs,
        out_specs=[pl.BlockSpec((tq, D_V), lambda h, i: (i, h))] + cast_specs,
        out_shape=[jax.ShapeDtypeStruct((s, width), BF16)]
                  + [jax.ShapeDtypeStruct(w.shape, BF16) for w in cast_weights],
        scratch_shapes=[pltpu.VMEM((s // tk, D_V + ONES_ROWS, tk), BF16),
                        pltpu.VMEM((1, 2 * tq), F32),
                        pltpu.VMEM((D_V + ONES_ROWS, 2 * tq), F32),
                        pltpu.VMEM((2, tk, 2 * tq), F32)],
        compiler_params=_params("arbitrary", "arbitrary"),
        name="flash_diff_attention",
    )(q, k, v, *lam_vecs, subln_g.reshape(D_V, 1), *cast_weights)
    return outs[0], outs[1:]


def _paged_update(p_idx, n_steps, q_ref, kn_ref, vn_ref, k_refs, v_refs, lam_refs, sg_ref, o_ref,
                  scratch, host_work, *, n_heads, lam_init):
    lq1_ref, lk1_ref, lq2_ref, lk2_ref = lam_refs
    q2_ref, bias_ref, m_ref, l_ref, acc_ref = scratch

    @pl.when(p_idx == 0)
    def _():
        q = q_ref[...].astype(F32)
        lane = lax.broadcasted_iota(jnp.int32, q.shape, 1)
        q2_ref[...] = jnp.concatenate([jnp.where(lane < D_SUB, q, 0.0),
                                       jnp.where(lane >= D_SUB, q, 0.0)], axis=0).astype(BF16)
        row = lax.broadcasted_iota(jnp.int32, bias_ref.shape, 0)
        col = lax.broadcasted_iota(jnp.int32, bias_ref.shape, 1)
        same_head = (col & (n_heads - 1)) == (row & (n_heads - 1))
        bias_ref[...] = jnp.where(same_head, 0.0, NEG_INF)
        m_ref[...] = jnp.full_like(m_ref, NEG_INF)
        l_ref[...] = jnp.zeros_like(l_ref)
        acc_ref[...] = jnp.zeros_like(acc_ref)

    q2 = q2_ref[...]
    bias = bias_ref[...]
    s_parts = [lax.dot_general(q2, k_ref[...].astype(BF16), (((1,), (1,)), ((), ())),
                               preferred_element_type=F32) + bias for k_ref in k_refs]
    host_work()
    m_prev = m_ref[...]
    m_new = m_prev
    for s in s_parts:
        m_new = jnp.maximum(m_new, jnp.max(s, axis=1, keepdims=True))
    alpha = jnp.exp(m_prev - m_new)
    l_new = alpha * l_ref[...]
    acc = alpha * acc_ref[...]
    for s, v_ref in zip(s_parts, v_refs):
        p = jnp.exp(s - m_new)
        l_new = l_new + jnp.sum(p, axis=1, keepdims=True)
        acc = acc + jnp.dot(p.astype(BF16), v_ref[...].astype(BF16), preferred_element_type=F32)
    l_ref[...] = l_new
    acc_ref[...] = acc
    m_ref[...] = m_new

    @pl.when(p_idx == n_steps - 1)
    def _():
        kn2 = jnp.concatenate([kn_ref[...], kn_ref[...]], axis=0)
        vn2 = jnp.concatenate([vn_ref[...], vn_ref[...]], axis=0)
        s_new = jnp.sum(q2_ref[...].astype(F32) * kn2, axis=1, keepdims=True)
        m_old = m_ref[...]
        m_fin = jnp.maximum(m_old, s_new)
        a = jnp.exp(m_old - m_fin)
        p_new = jnp.exp(s_new - m_fin)
        l = a * l_ref[...] + p_new
        acc_f = a * acc_ref[...] + p_new * vn2
        lam = _diff_lambda(lq1_ref, lk1_ref, lq2_ref, lk2_ref, lam_init)
        o = acc_f[:n_heads] / l[:n_heads] - lam * (acc_f[n_heads:] / l[n_heads:])
        ms = jnp.mean(o * o, axis=1, keepdims=True)
        y = o * lax.rsqrt(ms + EPS) * sg_ref[...] * (1.0 - lam_init)
        o_ref[...] = y.astype(o_ref.dtype)


def _mlp_paged_kernel(pt_ref, *refs, host, grid_dims, pp, steps_per_seq, n_heads, lam_init):
    del pt_ref
    n_host = 2 if host == "up" else 3
    host_in, rest = refs[:n_host], refs[n_host:]
    q_ref, kn_ref, vn_ref = rest[:3]
    k_refs, v_refs = rest[3:3 + pp], rest[3 + pp:3 + 2 * pp]
    lam_refs = rest[3 + 2 * pp:7 + 2 * pp]
    sg_ref, o_ref, attn_ref = rest[7 + 2 * pp:10 + 2 * pp]
    scratch = rest[10 + 2 * pp:]

    def host_work():
        if host == "up":
            part = jnp.dot(host_in[0][...], host_in[1][...], preferred_element_type=F32)
            a = jnp.maximum(part, 0.0)
            o_ref[...] = (a * a).astype(o_ref.dtype)
        else:
            @pl.when(pl.program_id(2) == 0)
            def _():
                o_ref[...] = host_in[2][...] + jnp.dot(host_in[0][...], host_in[1][...],
                                                      preferred_element_type=F32)

            @pl.when(pl.program_id(2) > 0)
            def _():
                o_ref[...] += jnp.dot(host_in[0][...], host_in[1][...], preferred_element_type=F32)

    step = pl.program_id(0)
    for axis in range(1, len(grid_dims)):
        step = step * grid_dims[axis] + pl.program_id(axis)
    _paged_update(lax.rem(step, steps_per_seq), steps_per_seq, q_ref, kn_ref, vn_ref, k_refs, v_refs,
                  lam_refs, sg_ref, attn_ref, scratch, host_work, n_heads=n_heads, lam_init=lam_init)


def mlp_matmul_with_paged_attention(host, x, w, resid, sample, seq_lo, n_seq, *, tm, tn, tk,
                                    pages_per_step):
    m, kdim = x.shape
    n = w.shape[1]
    pp = pages_per_step
    n_heads = sample["n_heads"]
    page_table = sample["page_table"]
    steps_per_seq = page_table.shape[1] // pp
    rows = 2 * n_heads
    page_rows = PAGE_SIZE * n_heads
    if host == "up":
        grid = (n // tn, m // tm)
        assert tk == kdim
        host_specs = [pl.BlockSpec((tm, kdim), lambda j, i, pt: (i, 0)),
                      pl.BlockSpec((kdim, tn), lambda j, i, pt: (0, j))]
        host_out = pl.BlockSpec((tm, tn), lambda j, i, pt: (i, j))
        host_args, out_dtype = (x, w), BF16
    else:
        grid = (m // tm, n // tn, kdim // tk)
        host_specs = [pl.BlockSpec((tm, tk), lambda i, j, k, pt: (i, k)),
                      pl.BlockSpec((tk, tn), lambda i, j, k, pt: (k, j)),
                      pl.BlockSpec((tm, tn), lambda i, j, k, pt: (i, j))]
        host_out = pl.BlockSpec((tm, tn), lambda i, j, k, pt: (i, j))
        host_args, out_dtype = (x, w, resid), F32
    n_steps = math.prod(grid)
    assert n_steps == n_seq * steps_per_seq, (grid, n_seq, steps_per_seq)

    def linear(ids):
        step = ids[0]
        for axis in range(1, len(grid)):
            step = step * grid[axis] + ids[axis]
        return step

    def seq_map(*args):
        return (seq_lo + linear(args[:-1]) // steps_per_seq, 0, 0)

    def page_map(c):
        def index(*args):
            step, pt = linear(args[:-1]), args[-1]
            seq = seq_lo + step // steps_per_seq
            return (sample["page_base"] + pt[seq, (step % steps_per_seq) * pp + c], 0, 0)
        return index

    def const_map(*args):
        return (0, 0)

    seq_spec = pl.BlockSpec((None, n_heads, D_V), seq_map)
    page_specs = [pl.BlockSpec((None, page_rows, D_V), page_map(c)) for c in range(pp)]
    vec = pl.BlockSpec((1, D_SUB), const_map)
    out, attn = pl.pallas_call(
        functools.partial(_mlp_paged_kernel, host=host, grid_dims=grid, pp=pp,
                          steps_per_seq=steps_per_seq, n_heads=n_heads, lam_init=sample["lam_init"]),
        grid_spec=pltpu.PrefetchScalarGridSpec(
            num_scalar_prefetch=1,
            grid=grid,
            in_specs=host_specs + [seq_spec, seq_spec, seq_spec] + page_specs + page_specs
                     + [vec, vec, vec, vec, pl.BlockSpec((1, D_V), const_map)],
            out_specs=[host_out,
                       pl.BlockSpec((None, n_heads, D_V),
                                    lambda *args: (linear(args[:-1]) // steps_per_seq, 0, 0))],
            scratch_shapes=[pltpu.VMEM((rows, D_V), BF16),
                            pltpu.VMEM((rows, page_rows), F32),
                            pltpu.VMEM((rows, 1), F32), pltpu.VMEM((rows, 1), F32),
                            pltpu.VMEM((rows, D_V), F32)]),
        out_shape=[jax.ShapeDtypeStruct((m, n), out_dtype),
                   jax.ShapeDtypeStruct((n_seq, n_heads, D_V), BF16)],
        compiler_params=_params(*(["arbitrary"] * len(grid))),
        name="mlp_" + host + "_paged_attention",
    )(page_table, *host_args, sample["q"], sample["k_new"], sample["v_new"],
      *([sample["cache_k"]] * pp), *([sample["cache_v"]] * pp), *sample["lam_vecs"],
      sample["subln_g"].reshape(1, D_V))
    return out, attn.reshape(n_seq, n_heads * D_V)


HALO = POOL_BUF + 1


def _pool_prompt_kernel(u_ref, halo_ref, w_ref, sc_ref, o_ref, ext_ref, *, tm, group):
    i = pl.program_id(0)
    halo = halo_ref[...]
    ext_ref[0:HALO, :] = jnp.where(i > 0, halo, jnp.zeros_like(halo))
    ext_ref[HALO:, :] = u_ref[...]
    pos = i * tm + lax.broadcasted_iota(jnp.int32, (tm, 1), 0)
    for g, win in enumerate(POOL_WINDOWS):
        cols = slice(g * group, (g + 1) * group)
        cur = u_ref[:, cols]
        tot = cur
        for k in range(1, win):
            tot = tot + ext_ref[HALO - k:HALO - k + tm, cols]
        cnt = jnp.minimum(win, pos + 1).astype(F32)
        d = tot / cnt - cur
        y = jnp.dot(d.astype(BF16), w_ref[g], preferred_element_type=F32) * sc_ref[:, cols]
        o_ref[:, cols] = y.astype(o_ref.dtype)


def pool_mix_prompt(u, pool_w, pool_scale, *, tm):
    s, width = u.shape
    group = width // len(POOL_WINDOWS)
    ratio = tm // HALO
    return pl.pallas_call(
        functools.partial(_pool_prompt_kernel, tm=tm, group=group),
        grid=(s // tm,),
        in_specs=[pl.BlockSpec((tm, width), lambda i: (i, 0)),
                  pl.BlockSpec((HALO, width), lambda i: (jnp.maximum(i * ratio - 1, 0), 0)),
                  pl.BlockSpec(pool_w.shape, lambda i: (0, 0, 0)),
                  pl.BlockSpec((1, width), lambda i: (0, 0))],
        out_specs=pl.BlockSpec((tm, width), lambda i: (i, 0)),
        out_shape=jax.ShapeDtypeStruct((s, width), BF16),
        scratch_shapes=[pltpu.VMEM((tm + HALO, width), F32)],
        compiler_params=_params("arbitrary"),
        name="pool_mix_prompt",
    )(u, u, pool_w, pool_scale.reshape(1, width))


def _pool_sample_kernel(st_ref, us_ref, w_ref, sc_ref, o_ref, *, group, first_pos):
    cur_all = us_ref[...]
    for g, win in enumerate(POOL_WINDOWS):
        cols = slice(g * group, (g + 1) * group)
        cur = cur_all[:, cols]
        tot = cur
        for k in range(1, win):
            tot = tot + st_ref[POOL_BUF - k, :, cols]
        cnt = float(min(win, first_pos + 1))
        d = tot / cnt - cur
        y = jnp.dot(d.astype(BF16), w_ref[g], preferred_element_type=F32) * sc_ref[:, cols]
        o_ref[:, cols] = y.astype(o_ref.dtype)


def pool_mix_sample(state_t, us, pool_w, pool_scale, first_pos):
    b, width = us.shape
    group = width // len(POOL_WINDOWS)
    return pl.pallas_call(
        functools.partial(_pool_sample_kernel, group=group, first_pos=first_pos),
        grid=(1,),
        in_specs=[pl.BlockSpec(state_t.shape, lambda i: (0, 0, 0)),
                  pl.BlockSpec((b, width), lambda i: (0, 0)),
                  pl.BlockSpec(pool_w.shape, lambda i: (0, 0, 0)),
                  pl.BlockSpec((1, width), lambda i: (0, 0))],
        out_specs=pl.BlockSpec((b, width), lambda i: (0, 0)),
        out_shape=jax.ShapeDtypeStruct((b, width), BF16),
        compiler_params=_params("arbitrary"),
        name="pool_mix_sample",
    )(state_t, us, pool_w, pool_scale.reshape(1, width))


def _block_diag_ones():
    r = jnp.arange(MXU_DIM) // D_SUB
    return (r[:, None] == r[None, :]).astype(BF16)


def _tile_lanes(vec, n):
    return jnp.tile(vec, n // vec.shape[0]).reshape(1, n)


def project(x, pos, wts, *, tm, tn, head_major=False):
    q_layouts = ("heads",) if head_major else ("rows",)
    kv_layouts = ("rows", "heads") if head_major else ("rows",)
    d = x.shape[1]
    attn_w = d // 2
    h = rmsnorm(x, wts["norm1_g"], min(tm, 256))
    cos, s1, s2 = rope_tables(pos, tm)
    tab_spec = pl.BlockSpec((tm, LANES), lambda n, i: (i, 0))
    vec_spec = pl.BlockSpec((1, tn), lambda n, i: (0, 0))
    bd_spec = pl.BlockSpec((MXU_DIM, MXU_DIM), lambda n, i: (0, 0))
    specs = (vec_spec, bd_spec, tab_spec, tab_spec, tab_spec)
    bd = _block_diag_ones()
    q = mm_fullk([h], wts["w_in"], 0, attn_w, tm=tm, tn=tn, out_dtype=BF16, epilogue="qk",
                 extra=(_tile_lanes(wts["q_norm_g"], tn), bd, cos, s1, s2), extra_specs=specs,
                 scale=D_SUB ** -0.5, layouts=q_layouts, name="proj_q")
    k = mm_fullk([h], wts["w_in"], attn_w, attn_w, tm=tm, tn=tn, out_dtype=F32, epilogue="qk",
                 extra=(_tile_lanes(wts["k_norm_g"], tn), bd, cos, s1, s2), extra_specs=specs,
                 layouts=kv_layouts, name="proj_k")
    v = mm_fullk([h], wts["w_in"], 2 * attn_w, attn_w, tm=tm, tn=tn, out_dtype=F32,
                 layouts=kv_layouts, name="proj_v")
    u = mm_fullk([h], wts["w_in"], 3 * attn_w, d - attn_w, tm=tm, tn=tn, out_dtype=F32, name="proj_u")
    return q, k, v, u


def finish(x, attn, pool, wts, plan, sample=None):
    tm, tn, tn_up, tm_down, tn_down, tk_down = (
        plan[key] for key in ("tm", "tn", "tn_up", "tm_down", "tn_down", "tk_down"))
    pages_per_step = plan.get("pages_per_step")
    d = x.shape[1]
    x1 = mm_fullk([attn, pool], wts["w_out"], 0, d, tm=tm, tn=tn, out_dtype=F32, epilogue="resid",
                  extra=(x,), extra_specs=(pl.BlockSpec((tm, tn), lambda n, i: (i, n)),),
                  name="out_proj")
    h2 = rmsnorm(x1, wts["norm2_g"], min(tm, 256))
    if sample is None:
        a = mm_fullk([h2], wts["w_up"], 0, wts["w_up"].shape[1], tm=tm, tn=tn_up, out_dtype=BF16,
                     epilogue="relu2", name="mlp_up")
        y = mm_ktiled_resid(a, wts["w_down"], x1, tm=tm_down, tn=tn_down, tk=tk_down, name="mlp_down")
        return y, None
    n_seq = sample["q"].shape[0]
    n_lo = n_seq // 2
    a, attn_lo = mlp_matmul_with_paged_attention(
        "up", h2, wts["w_up"], None, sample, 0, n_lo, tm=tm, tn=tn_up, tk=d,
        pages_per_step=pages_per_step)
    y, attn_hi = mlp_matmul_with_paged_attention(
        "down", a, wts["w_down"], x1, sample, n_lo, n_seq - n_lo, tm=tm_down, tn=tn_down, tk=tk_down,
        pages_per_step=pages_per_step)
    return y, jnp.concatenate([attn_lo, attn_hi], axis=0)


def kernel(x_prompt, x_sample, cache_k, cache_v, state_pool, page_table, norm1_g, w_in, q_norm_g,
           k_norm_g, lambda_q1, lambda_k1, lambda_q2, lambda_k2, subln_g, pool_w, pool_scale, w_out,
           norm2_g, w_up, w_down):
    batch, seq, d_model = x_prompt.shape
    dec_batch, dec_seq, _ = x_sample.shape
    assert batch == 1 and dec_seq == 1
    depth = w_in.shape[0]
    n_phys = cache_k.shape[1]
    n_heads = cache_k.shape[3]
    attn_w = n_heads * D_V
    pool_width = d_model - attn_w
    past_len = page_table.shape[1] * PAGE_SIZE

    xp = x_prompt.reshape(seq, d_model)
    xs = x_sample.reshape(dec_batch, d_model)
    pos_p = jnp.arange(seq, dtype=F32).reshape(seq, 1)
    pos_s = jnp.full((dec_batch, 1), past_len, F32)
    plan_p, plan_s = _tile_plan(seq, True), _tile_plan(dec_batch, False)

    outs = [[] for _ in range(6)]
    for l in range(depth):
        lam_init = 0.8 - 0.6 * math.exp(-0.3 * l)
        wts = {
            "norm1_g": norm1_g[l], "q_norm_g": q_norm_g[l], "k_norm_g": k_norm_g[l],
            "norm2_g": norm2_g[l],
            "w_in": w_in[l].astype(BF16),
        }
        pool_w_l = pool_w[l].astype(BF16)
        lam_vecs = [v[l].reshape(1, D_SUB) for v in (lambda_q1, lambda_k1, lambda_q2, lambda_k2)]

        q_hm, (k, k_hm), (v, v_hm), u = project(xp, pos_p, wts, tm=plan_p["tm"],
                                                tn=plan_p["tn_proj"], head_major=True)
        qs, kn, vn, us = project(xs, pos_s, wts, tm=plan_s["tm"], tn=plan_s["tn_proj"])
        to_heads = lambda a: a.reshape(dec_batch, n_heads, D_V)
        sample = {
            "n_heads": n_heads, "page_table": page_table, "page_base": l * n_phys,
            "q": to_heads(qs), "k_new": to_heads(kn), "v_new": to_heads(vn),
            "cache_k": cache_k.reshape(-1, PAGE_SIZE * n_heads, D_V),
            "cache_v": cache_v.reshape(-1, PAGE_SIZE * n_heads, D_V),
            "lam_vecs": lam_vecs, "subln_g": subln_g[l], "lam_init": lam_init,
        }

        attn, (wts["w_out"], wts["w_up"], wts["w_down"]) = flash_diff_attention(
            q_hm, k_hm, v_hm, lam_vecs, subln_g[l], lam_init, (w_out[l], w_up[l], w_down[l]),
            tq=plan_p["tq"], tk=plan_p["tk"])
        pool = pool_mix_prompt(u, pool_w_l, pool_scale[l], tm=plan_p["tm_pool"])
        xp, attn_s = finish(xp, attn, pool, wts, plan_p, sample=sample)
        outs[0].append(k.reshape(batch, seq, n_heads, D_V))
        outs[1].append(v.reshape(batch, seq, n_heads, D_V))
        outs[2].append(u[seq - POOL_BUF:].reshape(batch, POOL_BUF, pool_width))

        state_l = state_pool[l]
        pool_s = pool_mix_sample(jnp.swapaxes(state_l, 0, 1), us, pool_w_l, pool_scale[l], past_len)
        xs, _ = finish(xs, attn_s, pool_s, wts, plan_s)
        outs[3].append(kn.reshape(dec_batch, dec_seq, n_heads, D_V))
        outs[4].append(vn.reshape(dec_batch, dec_seq, n_heads, D_V))
        outs[5].append(jnp.concatenate([state_l[:, 1:], us[:, None, :]], axis=1))

    k_prompt, v_prompt, pool_prompt, k_sample, v_sample, pool_sample = (jnp.stack(o) for o in outs)
    return (xp.reshape(batch, seq, d_model), xs.reshape(dec_batch, dec_seq, d_model),
            k_prompt, v_prompt, pool_prompt, k_sample, v_sample, pool_sample)
```
